```python
import math
import jax, jax.numpy as jnp
from jax import lax
import numpy as np

D_MODEL = 1024
BATCH = 2
SEQ = 8192
DEPTH = 1

MLA_HEADS = 8
MLA_Q_RANK = 384
MLA_KV_RANK = 256
MLA_NOPE = 64
MLA_ROPE = 32
MLA_V = 64
MLA_SCALE = 1.0 / math.sqrt(MLA_NOPE + MLA_ROPE)
ROPE_THETA = 10000.0
FOX_HEADS = 8
FOX_DIM = 64
FOX_SCALE = 1.0 / math.sqrt(FOX_DIM)
D_FF = 2816
CONV_WIDTH = 3
BLOCK_Q = 128
EPS = 1e-6
N_ADA = 6

IN_SPLITS = (
    MLA_Q_RANK,
    MLA_KV_RANK,
    MLA_ROPE,
    FOX_HEADS * FOX_DIM,
    FOX_HEADS * FOX_DIM,
    FOX_HEADS * FOX_DIM,
    FOX_HEADS,
    D_MODEL,
    D_MODEL,
)
D_IN = sum(IN_SPLITS)
IN_OFFSETS = tuple(int(v) for v in np.cumsum(IN_SPLITS)[:-1])

kernel_name = "hybrid_mla_fox_convffn_adaln"


def rmsnorm(x, g):
    xf = x.astype(jnp.float32)
    y = xf * lax.rsqrt(jnp.mean(xf * xf, axis=-1, keepdims=True) + EPS)
    return (y * g.astype(jnp.float32)).astype(x.dtype)


def rope(x, positions):
    r = x.shape[-1]
    inv_freq = ROPE_THETA ** (-jnp.arange(0, r, 2, dtype=jnp.float32) / r)
    ang = positions.astype(jnp.float32)[..., None] * inv_freq
    cos = jnp.cos(ang)[:, :, None, :]
    sin = jnp.sin(ang)[:, :, None, :]
    xf = x.astype(jnp.float32)
    x1, x2 = xf[..., : r // 2], xf[..., r // 2:]
    out = jnp.concatenate([x1 * cos - x2 * sin, x2 * cos + x1 * sin], axis=-1)
    return out.astype(x.dtype)


def causal_block_attention(q, k, v, scale, log_cum=None):
    b, s, h, dk = q.shape
    nb = s // BLOCK_Q
    idx = jnp.arange(nb)
    qb = q.reshape(b, nb, BLOCK_Q, h, dk).transpose(1, 0, 2, 3, 4)
    key_pos = jnp.arange(s)
    if log_cum is not None:
        f_keys = log_cum.transpose(0, 2, 1)
        fb = log_cum.reshape(b, nb, BLOCK_Q, h).transpose(1, 0, 2, 3)
        xs = (idx, qb, fb)
    else:
        xs = (idx, qb)

    def one_block(args):
        i, qi = args[0], args[1]
        logits = jnp.einsum('bqhd,bkhd->bhqk', qi, k,
                            preferred_element_type=jnp.float32) * scale
        if log_cum is not None:
            fi = args[2].transpose(0, 2, 1)
            logits = logits + (fi[..., :, None] - f_keys[:, :, None, :])
        q_pos = i * BLOCK_Q + jnp.arange(BLOCK_Q)
        mask = key_pos[None, :] <= q_pos[:, None]
        logits = jnp.where(mask[None, None], logits, -jnp.inf)
        p = jax.nn.softmax(logits, axis=-1)
        return jnp.einsum('bhqk,bkhd->bqhd', p.astype(v.dtype), v)

    out = lax.map(one_block, xs)
    return out.transpose(1, 0, 2, 3, 4).reshape(b, s, h, v.shape[-1])


def hybrid_mixer(h, positions, w_in, q_norm_g, w_uq, kv_norm_g, w_ukv, b_forget,
                 w_o_mla, w_o_fox, w_out):
    b, s, _ = h.shape
    proj = h @ w_in
    cq, ckv, k_rope, fq, fk, fv, f_logit, g_mla, g_fox = jnp.split(proj, IN_OFFSETS, axis=-1)

    q = (rmsnorm(cq, q_norm_g) @ w_uq).reshape(b, s, MLA_HEADS, MLA_NOPE + MLA_ROPE)
    q_nope, q_rope = q[..., :MLA_NOPE], q[..., MLA_NOPE:]
    kv = (rmsnorm(ckv, kv_norm_g) @ w_ukv).reshape(b, s, MLA_HEADS, MLA_NOPE + MLA_V)
    k_nope, v_mla = kv[..., :MLA_NOPE], kv[..., MLA_NOPE:]
    q_rope = rope(q_rope, positions)
    k_rope = rope(k_rope[:, :, None, :], positions)
    q_mla = jnp.concatenate([q_nope, q_rope], axis=-1)
    k_mla = jnp.concatenate(
        [k_nope, jnp.broadcast_to(k_rope, (b, s, MLA_HEADS, MLA_ROPE))], axis=-1)
    o_mla = causal_block_attention(q_mla, k_mla, v_mla, MLA_SCALE).reshape(b, s, MLA_HEADS * MLA_V)

    log_f = jax.nn.log_sigmoid(f_logit.astype(jnp.float32) + b_forget.astype(jnp.float32))
    f_cum = jnp.cumsum(log_f, axis=1)
    o_fox = causal_block_attention(
        fq.reshape(b, s, FOX_HEADS, FOX_DIM), fk.reshape(b, s, FOX_HEADS, FOX_DIM),
        fv.reshape(b, s, FOX_HEADS, FOX_DIM), FOX_SCALE, f_cum).reshape(b, s, FOX_HEADS * FOX_DIM)

    y = jax.nn.sigmoid(g_mla) * (o_mla @ w_o_mla) + jax.nn.sigmoid(g_fox) * (o_fox @ w_o_fox)
    return y @ w_out


def conv_ffn(h, w_up, conv_w, conv_b, w_down):
    s = h.shape[1]
    u = h @ w_up
    up = jnp.pad(u, ((0, 0), (CONV_WIDTH - 1, 0), (0, 0)))
    u = conv_b + sum(conv_w[j] * up[:, j:j + s] for j in range(CONV_WIDTH))
    gate, val = u[..., :D_FF], u[..., D_FF:]
    return (jax.nn.silu(gate) * val) @ w_down


def modulate(hn, shift, scale):
    return hn * (1.0 + scale[:, None, :]) + shift[:, None, :]


def setup_inputs(seed: int = 0) -> dict:
    key = jax.random.key(seed)
    ks = jax.random.split(key, 24)
    nrm = lambda k, shape, fan: jax.random.normal(k, shape, jnp.float32) * (fan ** -0.5)
    L = DEPTH
    x = jax.random.normal(ks[0], (BATCH, SEQ, D_MODEL), jnp.float32)
    c = jax.random.normal(ks[1], (BATCH, D_MODEL), jnp.float32)
    offsets = jax.random.randint(ks[2], (BATCH, 1), 0, 4096, dtype=jnp.int32)
    positions = offsets + jnp.arange(SEQ, dtype=jnp.int32)[None, :]
    return {
        "x": x,
        "c": c,
        "positions": positions,
        "w_ada": 0.5 * nrm(ks[3], (L, D_MODEL, N_ADA * D_MODEL), D_MODEL),
        "b_ada": 0.01 * jax.random.normal(ks[4], (L, N_ADA * D_MODEL), jnp.float32),
        "norm_mix_g": 1.0 + 0.1 * jax.random.normal(ks[5], (L, D_MODEL), jnp.float32),
        "w_in": nrm(ks[6], (L, D_MODEL, D_IN), D_MODEL),
        "q_norm_g": 1.0 + 0.1 * jax.random.normal(ks[7], (L, MLA_Q_RANK), jnp.float32),
        "w_uq": nrm(ks[8], (L, MLA_Q_RANK, MLA_HEADS * (MLA_NOPE + MLA_ROPE)), MLA_Q_RANK),
        "kv_norm_g": 1.0 + 0.1 * jax.random.normal(ks[9], (L, MLA_KV_RANK), jnp.float32),
        "w_ukv": nrm(ks[10], (L, MLA_KV_RANK, MLA_HEADS * (MLA_NOPE + MLA_V)), MLA_KV_RANK),
        "b_forget": jax.random.uniform(ks[11], (L, FOX_HEADS), jnp.float32, 1.0, 6.0),
        "w_o_mla": nrm(ks[12], (L, MLA_HEADS * MLA_V, D_MODEL), MLA_HEADS * MLA_V),
        "w_o_fox": nrm(ks[13], (L, FOX_HEADS * FOX_DIM, D_MODEL), FOX_HEADS * FOX_DIM),
        "w_out": nrm(ks[14], (L, D_MODEL, D_MODEL), D_MODEL),
        "norm_ffn_g": 1.0 + 0.1 * jax.random.normal(ks[15], (L, D_MODEL), jnp.float32),
        "w_up": nrm(ks[16], (L, D_MODEL, 2 * D_FF), D_MODEL),
        "conv_w": nrm(ks[17], (L, CONV_WIDTH, 2 * D_FF), CONV_WIDTH),
        "conv_b": 0.01 * jax.random.normal(ks[18], (L, 2 * D_FF), jnp.float32),
        "w_down": nrm(ks[19], (L, D_FF, D_MODEL), D_FF),
        "norm_final_g": 1.0 + 0.1 * jax.random.normal(ks[20], (D_MODEL,), jnp.float32),
    }


def reference(x, c, positions, w_ada, b_ada, norm_mix_g, w_in, q_norm_g, w_uq, kv_norm_g,
              w_ukv, b_forget, w_o_mla, w_o_fox, w_out, norm_ffn_g, w_up, conv_w, conv_b,
              w_down, norm_final_g):
    c_act = jax.nn.silu(c)
    for l in range(DEPTH):
        ada = c_act @ w_ada[l] + b_ada[l]
        sh_m, sc_m, g_m, sh_f, sc_f, g_f = jnp.split(ada, N_ADA, axis=-1)
        h = modulate(rmsnorm(x, norm_mix_g[l]), sh_m, sc_m)
        mix = hybrid_mixer(h, positions, w_in[l], q_norm_g[l], w_uq[l], kv_norm_g[l], w_ukv[l],
                           b_forget[l], w_o_mla[l], w_o_fox[l], w_out[l])
        x = x + g_m[:, None, :] * mix
        h = modulate(rmsnorm(x, norm_ffn_g[l]), sh_f, sc_f)
        x = x + g_f[:, None, :] * conv_ffn(h, w_up[l], conv_w[l], conv_b[l], w_down[l])
    return rmsnorm(x, norm_final_g)
```

```python
import functools
import math

import jax
import jax.numpy as jnp
from jax import lax
from jax.experimental import pallas as pl
from jax.experimental.pallas import tpu as pltpu

F32 = jnp.float32
BF16 = jnp.bfloat16

D_MODEL = 1024
MLA_HEADS = 8
MLA_Q_RANK = 384
MLA_KV_RANK = 256
MLA_NOPE = 64
MLA_ROPE = 32
MLA_V = 64
MLA_SCALE = 1.0 / math.sqrt(MLA_NOPE + MLA_ROPE)
ROPE_THETA = 10000.0
FOX_HEADS = 8
FOX_DIM = 64
FOX_SCALE = 1.0 / math.sqrt(FOX_DIM)
D_FF = 2816
CONV_WIDTH = 3
EPS = 1e-6
N_ADA = 6

LANES = 128
HEAD_PAD = LANES
N_HEADS = MLA_HEADS + FOX_HEADS
ROPE_HALF = MLA_ROPE // 2
ROPE_LO = MLA_NOPE
ROPE_HI = MLA_NOPE + ROPE_HALF
FLOGIT_LO = MLA_NOPE + MLA_ROPE
AUG_LO = FOX_DIM

VMEM_LIMIT = 56 * 1024 * 1024

_COL_CQ = 0
_COL_CKV = _COL_CQ + MLA_Q_RANK
_COL_MISC = _COL_CKV + MLA_KV_RANK
_COL_FQ = _COL_MISC + LANES
_COL_FK = _COL_FQ + FOX_HEADS * HEAD_PAD
_COL_FV = _COL_FK + FOX_HEADS * HEAD_PAD
_COL_GATE = _COL_FV + FOX_HEADS * FOX_DIM
_COL_END = _COL_GATE + 2 * D_MODEL


def _silu(x):
    return x * (1.0 / (1.0 + jnp.exp(-x)))


def _sigmoid(x):
    return 1.0 / (1.0 + jnp.exp(-x))


def _rms_scale(x):
    return x * lax.rsqrt(jnp.mean(x * x, axis=-1, keepdims=True) + EPS)


def _ada_kernel(ct_ref, w_ref, b_ref, o_ref, *, batch):
    act = _silu(ct_ref[...])
    w = w_ref[...]
    for b in range(batch):
        o_ref[b:b + 1, :] = jnp.sum(w * act[:, b:b + 1], axis=0, keepdims=True) + b_ref[...]


def _ada_call(c, w_ada, b_ada):
    batch, d = c.shape
    n = w_ada.shape[1]
    tn = 1536
    return pl.pallas_call(
        functools.partial(_ada_kernel, batch=batch),
        grid=(n // tn,),
        in_specs=[pl.BlockSpec((d, batch), lambda j: (0, 0)),
                  pl.BlockSpec((d, tn), lambda j: (0, j)),
                  pl.BlockSpec((1, tn), lambda j: (0, j))],
        out_specs=pl.BlockSpec((batch, tn), lambda j: (0, j)),
        out_shape=jax.ShapeDtypeStruct((batch, n), F32),
        compiler_params=pltpu.CompilerParams(dimension_semantics=("arbitrary",),
                                             vmem_limit_bytes=VMEM_LIMIT),
        name="ada",
    )(c.T, w_ada, b_ada.reshape(1, n))


def _rope_tables(pos_col, freq_row):
    ang = pos_col * freq_row
    lane = lax.broadcasted_iota(jnp.int32, ang.shape, 1)
    cos = jnp.cos(ang)
    sin = jnp.sin(ang)
    s_lo = jnp.where((lane >= ROPE_LO) & (lane < ROPE_HI), -sin, 0.0)
    s_hi = jnp.where((lane >= ROPE_HI) & (lane < FLOGIT_LO), sin, 0.0)
    return cos, s_lo, s_hi


def _rope_block(t, cos, s_lo, s_hi):
    return (t * cos + pltpu.roll(t, LANES - ROPE_HALF, 1) * s_lo
            + pltpu.roll(t, ROPE_HALF, 1) * s_hi)


def _split3(x):
    hi = x.astype(BF16)
    r1 = x - hi.astype(F32)
    mid = r1.astype(BF16)
    lo = (r1 - mid.astype(F32)).astype(BF16)
    return hi, mid, lo


def _modulated_norm(x, gain_ref, sc_ref, sh_ref):
    return (_rms_scale(x) * (gain_ref[...] * (1.0 + sc_ref[0])) + sh_ref[0]).astype(BF16)


def _inproj_kernel(x_ref, pos_ref, ng_ref, sc_ref, sh_ref, freq_ref, bf_ref, qg_ref, kvg_ref,
                   wa_ref, wuq_ref, wukvk_ref, wukvv_ref,
                   q_ref, k_ref, v_ref, g_ref, carry_ref, *, tm):
    s_idx = pl.program_id(1)

    @pl.when(s_idx == 0)
    def _():
        carry_ref[...] = jnp.zeros_like(carry_ref)

    h = _modulated_norm(x_ref[0], ng_ref, sc_ref, sh_ref)

    def proj(lo, hi):
        return jnp.dot(h, wa_ref[:, lo:hi], preferred_element_type=F32)

    lane = lax.broadcasted_iota(jnp.int32, (tm, LANES), 1)
    cos, s_lo, s_hi = _rope_tables(pos_ref[0], freq_ref[...])

    cqn = (_rms_scale(proj(_COL_CQ, _COL_CKV)) * qg_ref[...]).astype(BF16)
    qm = jnp.dot(cqn, wuq_ref[...], preferred_element_type=F32)
    for hd in range(MLA_HEADS):
        blk = qm[:, hd * HEAD_PAD:(hd + 1) * HEAD_PAD]
        q_ref[0, :, hd * HEAD_PAD:(hd + 1) * HEAD_PAD] = (
            _rope_block(blk, cos, s_lo, s_hi) * MLA_SCALE).astype(BF16)

    misc = proj(_COL_MISC, _COL_FQ)
    krope = _rope_block(jnp.where(lane < FLOGIT_LO, misc, 0.0), cos, s_lo, s_hi)
    ckvn = (_rms_scale(proj(_COL_CKV, _COL_MISC)) * kvg_ref[...]).astype(BF16)
    kn = jnp.dot(ckvn, wukvk_ref[...], preferred_element_type=F32)
    for hd in range(MLA_HEADS):
        k_ref[0, :, hd * HEAD_PAD:(hd + 1) * HEAD_PAD] = (
            kn[:, hd * HEAD_PAD:(hd + 1) * HEAD_PAD] + krope).astype(BF16)
    v_ref[0, :, 0:MLA_HEADS * MLA_V] = jnp.dot(
        ckvn, wukvv_ref[...], preferred_element_type=F32).astype(BF16)

    z = misc + bf_ref[...]
    logf = jnp.minimum(z, 0.0) - jnp.log1p(jnp.exp(-jnp.abs(z)))
    logf = jnp.where((lane >= FLOGIT_LO) & (lane < FLOGIT_LO + FOX_HEADS), logf, 0.0)
    row = lax.broadcasted_iota(jnp.int32, (tm, tm), 0)
    col = lax.broadcasted_iota(jnp.int32, (tm, tm), 1)
    tri = jnp.where(col <= row, 1.0, 0.0).astype(BF16)
    parts = _split3(logf)
    fcum = carry_ref[...] + sum(jnp.dot(tri, p, preferred_element_type=F32) for p in parts)
    carry_ref[...] = fcum[tm - 1:tm, :]

    fq = proj(_COL_FQ, _COL_FK)
    fk = proj(_COL_FK, _COL_FV)
    for hd in range(FOX_HEADS):
        fb = jnp.broadcast_to(fcum[:, FLOGIT_LO + hd:FLOGIT_LO + hd + 1], (tm, LANES))
        hi, mid, lo = (p.astype(F32) for p in _split3(fb))
        q_aug = jnp.where(lane == AUG_LO, hi, jnp.where(lane == AUG_LO + 1, mid,
                jnp.where(lane == AUG_LO + 2, lo,
                jnp.where((lane >= AUG_LO + 3) & (lane < AUG_LO + 6), 1.0, 0.0))))
        k_aug = jnp.where(lane == AUG_LO + 3, -hi, jnp.where(lane == AUG_LO + 4, -mid,
                jnp.where(lane == AUG_LO + 5, -lo,
                jnp.where((lane >= AUG_LO) & (lane < AUG_LO + 3), 1.0, 0.0))))
        c0 = (MLA_HEADS + hd) * HEAD_PAD
        q_ref[0, :, c0:c0 + HEAD_PAD] = (
            fq[:, hd * HEAD_PAD:(hd + 1) * HEAD_PAD] * FOX_SCALE + q_aug).astype(BF16)
        k_ref[0, :, c0:c0 + HEAD_PAD] = (
            fk[:, hd * HEAD_PAD:(hd + 1) * HEAD_PAD] + k_aug).astype(BF16)
    v_ref[0, :, MLA_HEADS * MLA_V:] = proj(_COL_FV, _COL_GATE).astype(BF16)

    g_ref[0] = _sigmoid(proj(_COL_GATE, _COL_END)).astype(BF16)


def _inproj_call(x, pos_f, norm_g, sc_mix, sh_mix, freq_row, bf_row, qg, kvg, wa, wuq, wukvk,
                 wukvv, tm):
    b, s, d = x.shape
    const = lambda shape: pl.BlockSpec(shape, lambda bi, si: (0,) * len(shape))
    tok = lambda w: pl.BlockSpec((1, tm, w), lambda bi, si: (bi, si, 0))
    vec = pl.BlockSpec((1, 1, d), lambda bi, si: (bi, 0, 0))
    return pl.pallas_call(
        functools.partial(_inproj_kernel, tm=tm),
        grid=(b, s // tm),
        in_specs=[tok(d), tok(1), const((1, d)), vec, vec, const((1, LANES)), const((1, LANES)),
                  const((1, MLA_Q_RANK)), const((1, MLA_KV_RANK)),
                  const(wa.shape), const(wuq.shape), const(wukvk.shape), const(wukvv.shape)],
        out_specs=[tok(N_HEADS * HEAD_PAD), tok(N_HEADS * HEAD_PAD),
                   tok(N_HEADS * FOX_DIM), tok(2 * d)],
        out_shape=[jax.ShapeDtypeStruct((b, s, N_HEADS * HEAD_PAD), BF16),
                   jax.ShapeDtypeStruct((b, s, N_HEADS * HEAD_PAD), BF16),
                   jax.ShapeDtypeStruct((b, s, N_HEADS * FOX_DIM), BF16),
                   jax.ShapeDtypeStruct((b, s, 2 * d), BF16)],
        scratch_shapes=[pltpu.VMEM((1, LANES), F32)],
        compiler_params=pltpu.CompilerParams(dimension_semantics=("arbitrary", "arbitrary"),
                                             vmem_limit_bytes=VMEM_LIMIT),
        name="inproj",
    )(x, pos_f, norm_g, sc_mix, sh_mix, freq_row, bf_row, qg, kvg, wa, wuq, wukvk, wukvv)


def _attn_kernel(q_ref, k_ref, v_ref, o_ref, *, tq):
    qi = pl.program_id(2)
    row = lax.broadcasted_iota(jnp.int32, (tq, tq), 0)
    col = lax.broadcasted_iota(jnp.int32, (tq, tq), 1)
    causal = col <= row
    outs = []
    for hh in range(2):
        q = q_ref[0, :, hh * HEAD_PAD:(hh + 1) * HEAD_PAD]

        def step(j, carry, masked, q=q, hh=hh):
            m, l, acc = carry
            start = pl.multiple_of(j * tq, tq)
            k = k_ref[0, pl.ds(start, tq), hh * HEAD_PAD:(hh + 1) * HEAD_PAD]
            v = v_ref[0, pl.ds(start, tq), :]
            s = lax.dot_general(q, k, (((1,), (1,)), ((), ())), preferred_element_type=F32)
            if masked:
                s = jnp.where(causal, s, -jnp.inf)
            m_new = jnp.maximum(m, jnp.max(s, axis=1, keepdims=True))
            alpha = jnp.exp(m - m_new)
            p = jnp.exp(s - m_new)
            l = alpha * l + jnp.sum(p, axis=1, keepdims=True)
            acc = alpha * acc + jnp.dot(p.astype(BF16), v, preferred_element_type=F32)
            return m_new, l, acc

        init = (jnp.full((tq, 1), -jnp.inf, F32), jnp.zeros((tq, 1), F32),
                jnp.zeros((tq, LANES), F32))
        carry = lax.fori_loop(0, qi, functools.partial(step, masked=False), init)
        m, l, acc = step(qi, carry, masked=True)
        outs.append(acc / l)
    lane = lax.broadcasted_iota(jnp.int32, (tq, LANES), 1)
    o_ref[0] = jnp.where(lane < MLA_V, outs[0], outs[1]).astype(BF16)


def _attn_call(q_all, k_all, v_all, tq):
    b, s, _ = q_all.shape
    pairs = N_HEADS // 2
    return pl.pallas_call(
        functools.partial(_attn_kernel, tq=tq),
        grid=(b, pairs, s // tq),
        in_specs=[pl.BlockSpec((1, tq, 2 * HEAD_PAD), lambda bi, pi, qi: (bi, qi, pi)),
                  pl.BlockSpec((1, s, 2 * HEAD_PAD), lambda bi, pi, qi: (bi, 0, pi)),
                  pl.BlockSpec((1, s, LANES), lambda bi, pi, qi: (bi, 0, pi))],
        out_specs=pl.BlockSpec((1, tq, LANES), lambda bi, pi, qi: (bi, qi, pi)),
        out_shape=jax.ShapeDtypeStruct((b, s, pairs * LANES), BF16),
        compiler_params=pltpu.CompilerParams(
            dimension_semantics=("arbitrary", "arbitrary", "arbitrary"),
            vmem_limit_bytes=VMEM_LIMIT),
        name="attn",
    )(q_all, k_all, v_all)


def _outproj_kernel(o_ref, g_ref, x_ref, gm_ref, wm_ref, wf_ref, wo_ref, x1_ref):
    half = MLA_HEADS * MLA_V
    o = o_ref[0]
    g = g_ref[0].astype(F32)
    y = (g[:, :D_MODEL] * jnp.dot(o[:, :half], wm_ref[...], preferred_element_type=F32)
         + g[:, D_MODEL:] * jnp.dot(o[:, half:], wf_ref[...], preferred_element_type=F32))
    mix = jnp.dot(y.astype(BF16), wo_ref[...], preferred_element_type=F32)
    x1_ref[0] = x_ref[0] + gm_ref[0] * mix


def _outproj_call(o_all, g_all, x, g_m, wm, wf, wo, tm):
    b, s, d = x.shape
    const = lambda shape: pl.BlockSpec(shape, lambda bi, si: (0,) * len(shape))
    tok = lambda w: pl.BlockSpec((1, tm, w), lambda bi, si: (bi, si, 0))
    vec = pl.BlockSpec((1, 1, d), lambda bi, si: (bi, 0, 0))
    return pl.pallas_call(
        _outproj_kernel,
        grid=(b, s // tm),
        in_specs=[tok(o_all.shape[-1]), tok(g_all.shape[-1]), tok(d), vec,
                  const(wm.shape), const(wf.shape), const(wo.shape)],
        out_specs=tok(d),
        out_shape=jax.ShapeDtypeStruct((b, s, d), F32),
        compiler_params=pltpu.CompilerParams(dimension_semantics=("arbitrary", "arbitrary"),
                                             vmem_limit_bytes=VMEM_LIMIT),
        name="outproj",
    )(o_all, g_all, x, g_m, wm, wf, wo)


def _shift_rows(u, prev, k):
    tm = u.shape[0]
    body = pltpu.roll(u, k, 0)
    head = jnp.where(lax.broadcasted_iota(jnp.int32, prev.shape, 0) < k,
                     pltpu.roll(prev, k, 0), body[0:8, :])
    return jnp.concatenate([head, body[8:tm, :]], axis=0)


def _ffn_kernel(x1_ref, ng_ref, sc_ref, sh_ref, gf_ref, cw_ref, cb_ref, fg_ref, wup_ref, wdn_ref,
                out_ref, tail_ref, act_ref, *, chunk):
    s_idx = pl.program_id(1)

    @pl.when(s_idx == 0)
    def _():
        tail_ref[...] = jnp.zeros_like(tail_ref)

    x1 = x1_ref[0]
    tm = x1.shape[0]
    h = _modulated_norm(x1, ng_ref, sc_ref, sh_ref)

    def conv(c0):
        u = jnp.dot(h, wup_ref[:, c0:c0 + chunk], preferred_element_type=F32)
        prev = tail_ref[:, c0:c0 + chunk]
        tail_ref[:, c0:c0 + chunk] = u[tm - 8:tm, :]
        w = cw_ref[:, c0:c0 + chunk]
        return (cb_ref[:, c0:c0 + chunk] + w[0:1] * _shift_rows(u, prev, 2)
                + w[1:2] * _shift_rows(u, prev, 1) + w[2:3] * u)

    for c0 in range(0, D_FF, chunk):
        act_ref[:, c0:c0 + chunk] = (_silu(conv(c0)) * conv(D_FF + c0)).astype(BF16)

    y = jnp.dot(act_ref[...], wdn_ref[...], preferred_element_type=F32)
    x2 = x1 + gf_ref[0] * y
    out_ref[0] = _rms_scale(x2) * fg_ref[...]


def _ffn_call(x1, norm_g, sc_ffn, sh_ffn, g_f, conv_w, conv_b, final_g, wup, wdn, tm, chunk):
    b, s, d = x1.shape
    const = lambda shape: pl.BlockSpec(shape, lambda bi, si: (0,) * len(shape))
    tok = pl.BlockSpec((1, tm, d), lambda bi, si: (bi, si, 0))
    vec = pl.BlockSpec((1, 1, d), lambda bi, si: (bi, 0, 0))
    return pl.pallas_call(
        functools.partial(_ffn_kernel, chunk=chunk),
        grid=(b, s // tm),
        in_specs=[tok, const((1, d)), vec, vec, vec, const(conv_w.shape), const(conv_b.shape),
                  const(final_g.shape), const(wup.shape), const(wdn.shape)],
        out_specs=tok,
        out_shape=jax.ShapeDtypeStruct((b, s, d), F32),
        scratch_shapes=[pltpu.VMEM((8, 2 * D_FF), F32), pltpu.VMEM((tm, D_FF), BF16)],
        compiler_params=pltpu.CompilerParams(dimension_semantics=("arbitrary", "arbitrary"),
                                             vmem_limit_bytes=VMEM_LIMIT),
        name="ffn",
    )(x1, norm_g, sc_ffn, sh_ffn, g_f, conv_w, conv_b, final_g, wup, wdn)


def _pad_heads(w, heads, width):
    k = w.shape[0]
    w = w.reshape(k, heads, width)
    return jnp.pad(w, ((0, 0), (0, 0), (0, HEAD_PAD - width))).reshape(k, heads * HEAD_PAD)


def _prep_w_in(w_in):
    d = w_in.shape[0]
    splits = (MLA_Q_RANK, MLA_KV_RANK, MLA_ROPE, FOX_HEADS * FOX_DIM, FOX_HEADS * FOX_DIM,
              FOX_HEADS * FOX_DIM, FOX_HEADS, D_MODEL, D_MODEL)
    offs = [0]
    for n in splits:
        offs.append(offs[-1] + n)
    cq, ckv, krope, fq, fk, fv, flog, gm, gf = (w_in[:, offs[i]:offs[i + 1]] for i in range(9))
    misc = jnp.concatenate([jnp.zeros((d, MLA_NOPE), w_in.dtype), krope, flog,
                            jnp.zeros((d, LANES - FLOGIT_LO - FOX_HEADS), w_in.dtype)], axis=1)
    return jnp.concatenate([cq, ckv, misc, _pad_heads(fq, FOX_HEADS, FOX_DIM),
                            _pad_heads(fk, FOX_HEADS, FOX_DIM), fv, gm, gf], axis=1).astype(BF16)


def kernel(x, c, positions, w_ada, b_ada, norm_mix_g, w_in, q_norm_g, w_uq, kv_norm_g, w_ukv,
           b_forget, w_o_mla, w_o_fox, w_out, norm_ffn_g, w_up, conv_w, conv_b, w_down,
           norm_final_g):
    assert w_ada.shape[0] == 1, "single-layer block"
    b, s, d = x.shape
    ada = _ada_call(c, w_ada[0], b_ada[0])
    sh_m, sc_m, g_m, sh_f, sc_f, g_f = (ada[:, i * d:(i + 1) * d].reshape(b, 1, d)
                                        for i in range(N_ADA))

    inv_freq = ROPE_THETA ** (-jnp.arange(0, MLA_ROPE, 2, dtype=F32) / MLA_ROPE)
    freq_row = jnp.concatenate([jnp.zeros((MLA_NOPE,), F32), inv_freq, inv_freq,
                                jnp.zeros((LANES - FLOGIT_LO,), F32)]).reshape(1, LANES)
    bf_row = jnp.concatenate([jnp.zeros((FLOGIT_LO,), F32), b_forget[0].astype(F32),
                              jnp.zeros((LANES - FLOGIT_LO - FOX_HEADS,), F32)]).reshape(1, LANES)
    pos_f = positions.astype(F32).reshape(b, s, 1)

    wa = _prep_w_in(w_in[0])
    wuq = _pad_heads(w_uq[0], MLA_HEADS, MLA_NOPE + MLA_ROPE).astype(BF16)
    wukv = w_ukv[0].reshape(MLA_KV_RANK, MLA_HEADS, MLA_NOPE + MLA_V)
    wukvk = _pad_heads(wukv[:, :, :MLA_NOPE].reshape(MLA_KV_RANK, -1), MLA_HEADS,
                       MLA_NOPE).astype(BF16)
    wukvv = wukv[:, :, MLA_NOPE:].reshape(MLA_KV_RANK, -1).astype(BF16)

    q_all, k_all, v_all, g_all = _inproj_call(
        x, pos_f, norm_mix_g[0].reshape(1, d), sc_m, sh_m, freq_row, bf_row,
        q_norm_g[0].reshape(1, -1),
        kv_norm_g[0].reshape(1, -1), wa, wuq, wukvk, wukvv, tm=256)
    o_all = _attn_call(q_all, k_all, v_all, tq=512)
    x1 = _outproj_call(o_all, g_all, x, g_m, w_o_mla[0].astype(BF16), w_o_fox[0].astype(BF16),
                       w_out[0].astype(BF16), tm=512)
    return _ffn_call(x1, norm_ffn_g[0].reshape(1, d), sc_f, sh_f, g_f, conv_w[0],
                     conv_b[0].reshape(1, -1),
                     norm_final_g.reshape(1, -1), w_up[0].astype(BF16), w_down[0].astype(BF16),
                     tm=256, chunk=256)
```

```python
import functools
import math

import jax
import jax.numpy as jnp
from jax import lax
from jax.experimental import pallas as pl
from jax.experimental.pallas import tpu as pltpu

F32 = jnp.float32
BF16 = jnp.bfloat16

D_MODEL = 1024
MLA_HEADS = 8
MLA_Q_RANK = 384
MLA_KV_RANK = 256
MLA_NOPE = 64
MLA_ROPE = 32
MLA_V = 64
MLA_SCALE = 1.0 / math.sqrt(MLA_NOPE + MLA_ROPE)
ROPE_THETA = 10000.0
FOX_HEADS = 8
FOX_DIM = 64
FOX_SCALE = 1.0 / math.sqrt(FOX_DIM)
LOG2E = math.log2(math.e)
D_FF = 2816
CONV_WIDTH = 3
EPS = 1e-6
N_ADA = 6

LANES = 128
HEAD_PAD = LANES
N_HEADS = MLA_HEADS + FOX_HEADS
ROPE_HALF = MLA_ROPE // 2
ROPE_LO = MLA_NOPE
ROPE_HI = MLA_NOPE + ROPE_HALF
FLOGIT_LO = MLA_NOPE + MLA_ROPE
AUG_LO = FOX_DIM
ONES_ROWS = 16

VMEM_LIMIT = 56 * 1024 * 1024

_COL_CQ = 0
_COL_CKV = _COL_CQ + MLA_Q_RANK
_COL_MISC = _COL_CKV + MLA_KV_RANK
_COL_FQ = _COL_MISC + LANES
_COL_FK = _COL_FQ + FOX_HEADS * HEAD_PAD
_COL_FV = _COL_FK + FOX_HEADS * HEAD_PAD
_COL_GATE = _COL_FV + FOX_HEADS * FOX_DIM
_COL_END = _COL_GATE + 2 * D_MODEL


def _silu(x):
    return x * (1.0 / (1.0 + jnp.exp(-x)))


def _sigmoid(x):
    return 1.0 / (1.0 + jnp.exp(-x))


def _rms_scale(x):
    return x * lax.rsqrt(jnp.mean(x * x, axis=-1, keepdims=True) + EPS)


def _ada_kernel(ct_ref, w_ref, b_ref, o_ref, *, batch):
    act = _silu(ct_ref[...])
    w = w_ref[...]
    for b in range(batch):
        o_ref[b:b + 1, :] = jnp.sum(w * act[:, b:b + 1], axis=0, keepdims=True) + b_ref[...]


def _ada_call(c, w_ada, b_ada):
    batch, d = c.shape
    n = w_ada.shape[1]
    tn = 1536
    return pl.pallas_call(
        functools.partial(_ada_kernel, batch=batch),
        grid=(n // tn,),
        in_specs=[pl.BlockSpec((d, batch), lambda j: (0, 0)),
                  pl.BlockSpec((d, tn), lambda j: (0, j)),
                  pl.BlockSpec((1, tn), lambda j: (0, j))],
        out_specs=pl.BlockSpec((batch, tn), lambda j: (0, j)),
        out_shape=jax.ShapeDtypeStruct((batch, n), F32),
        compiler_params=pltpu.CompilerParams(dimension_semantics=("arbitrary",),
                                             vmem_limit_bytes=VMEM_LIMIT),
        name="ada",
    )(c.T, w_ada, b_ada.reshape(1, n))


def _rope_tables(pos_col, freq_row):
    ang = pos_col * freq_row
    lane = lax.broadcasted_iota(jnp.int32, ang.shape, 1)
    cos = jnp.cos(ang)
    sin = jnp.sin(ang)
    s_lo = jnp.where((lane >= ROPE_LO) & (lane < ROPE_HI), -sin, 0.0)
    s_hi = jnp.where((lane >= ROPE_HI) & (lane < FLOGIT_LO), sin, 0.0)
    return cos, s_lo, s_hi


def _rope_block(t, cos, s_lo, s_hi):
    return (t * cos + pltpu.roll(t, LANES - ROPE_HALF, 1) * s_lo
            + pltpu.roll(t, ROPE_HALF, 1) * s_hi)


def _split3(x):
    hi = x.astype(BF16)
    r1 = x - hi.astype(F32)
    mid = r1.astype(BF16)
    lo = (r1 - mid.astype(F32)).astype(BF16)
    return hi, mid, lo


def _modulated_norm(x, gain_ref, sc_ref, sh_ref):
    return (_rms_scale(x) * (gain_ref[...] * (1.0 + sc_ref[0])) + sh_ref[0]).astype(BF16)


def _inproj_kernel(x_ref, pos_ref, ng_ref, sc_ref, sh_ref, freq_ref, bf_ref, qg_ref, kvg_ref,
                   wa_ref, wuq_ref, wukvk_ref, wukvv_ref,
                   q_ref, k_ref, vt_ref, g_ref, carry_ref, *, tm):
    s_idx = pl.program_id(1)

    @pl.when(s_idx == 0)
    def _():
        carry_ref[...] = jnp.zeros_like(carry_ref)

    h = _modulated_norm(x_ref[0], ng_ref, sc_ref, sh_ref)

    def proj(lo, hi):
        return jnp.dot(h, wa_ref[:, lo:hi], preferred_element_type=F32)

    lane = lax.broadcasted_iota(jnp.int32, (tm, LANES), 1)
    cos, s_lo, s_hi = _rope_tables(pos_ref[0], freq_ref[...])

    cqn = (_rms_scale(proj(_COL_CQ, _COL_CKV)) * qg_ref[...]).astype(BF16)
    qm = jnp.dot(cqn, wuq_ref[...], preferred_element_type=F32)
    for hd in range(MLA_HEADS):
        blk = qm[:, hd * HEAD_PAD:(hd + 1) * HEAD_PAD]
        q_ref[0, :, hd * HEAD_PAD:(hd + 1) * HEAD_PAD] = (
            _rope_block(blk, cos, s_lo, s_hi) * (MLA_SCALE * LOG2E)).astype(BF16)

    misc = proj(_COL_MISC, _COL_FQ)
    krope = _rope_block(jnp.where(lane < FLOGIT_LO, misc, 0.0), cos, s_lo, s_hi)
    ckvn = (_rms_scale(proj(_COL_CKV, _COL_MISC)) * kvg_ref[...]).astype(BF16)
    kn = jnp.dot(ckvn, wukvk_ref[...], preferred_element_type=F32)
    for hd in range(MLA_HEADS):
        k_ref[0, :, hd * HEAD_PAD:(hd + 1) * HEAD_PAD] = (
            kn[:, hd * HEAD_PAD:(hd + 1) * HEAD_PAD] + krope).astype(BF16)
    vt_ref[0, 0:MLA_HEADS * MLA_V, :] = jnp.dot(
        ckvn, wukvv_ref[...], preferred_element_type=F32).T.astype(BF16)

    z = misc + bf_ref[...]
    logf = jnp.minimum(z, 0.0) - jnp.log1p(jnp.exp(-jnp.abs(z)))
    logf = jnp.where((lane >= FLOGIT_LO) & (lane < FLOGIT_LO + FOX_HEADS), logf, 0.0)
    row = lax.broadcasted_iota(jnp.int32, (tm, tm), 0)
    col = lax.broadcasted_iota(jnp.int32, (tm, tm), 1)
    tri = jnp.where(col <= row, 1.0, 0.0).astype(BF16)
    parts = _split3(logf)
    fcum = carry_ref[...] + sum(jnp.dot(tri, p, preferred_element_type=F32) for p in parts)
    carry_ref[...] = fcum[tm - 1:tm, :]

    fq = proj(_COL_FQ, _COL_FK)
    fk = proj(_COL_FK, _COL_FV)
    for hd in range(FOX_HEADS):
        fb = jnp.broadcast_to(fcum[:, FLOGIT_LO + hd:FLOGIT_LO + hd + 1], (tm, LANES)) * LOG2E
        hi, mid, lo = (p.astype(F32) for p in _split3(fb))
        q_aug = jnp.where(lane == AUG_LO, hi, jnp.where(lane == AUG_LO + 1, mid,
                jnp.where(lane == AUG_LO + 2, lo,
                jnp.where((lane >= AUG_LO + 3) & (lane < AUG_LO + 6), 1.0, 0.0))))
        k_aug = jnp.where(lane == AUG_LO + 3, -hi, jnp.where(lane == AUG_LO + 4, -mid,
                jnp.where(lane == AUG_LO + 5, -lo,
                jnp.where((lane >= AUG_LO) & (lane < AUG_LO + 3), 1.0, 0.0))))
        c0 = (MLA_HEADS + hd) * HEAD_PAD
        q_ref[0, :, c0:c0 + HEAD_PAD] = (
            fq[:, hd * HEAD_PAD:(hd + 1) * HEAD_PAD] * (FOX_SCALE * LOG2E) + q_aug).astype(BF16)
        k_ref[0, :, c0:c0 + HEAD_PAD] = (
            fk[:, hd * HEAD_PAD:(hd + 1) * HEAD_PAD] + k_aug).astype(BF16)
    vt_ref[0, MLA_HEADS * MLA_V:, :] = proj(_COL_FV, _COL_GATE).T.astype(BF16)

    g_ref[0] = _sigmoid(proj(_COL_GATE, _COL_END)).astype(BF16)


def _inproj_call(x, pos_f, norm_g, sc_mix, sh_mix, freq_row, bf_row, qg, kvg, wa, wuq, wukvk,
                 wukvv, tm):
    b, s, d = x.shape
    const = lambda shape: pl.BlockSpec(shape, lambda bi, si: (0,) * len(shape))
    tok = lambda w: pl.BlockSpec((1, tm, w), lambda bi, si: (bi, si, 0))
    vec = pl.BlockSpec((1, 1, d), lambda bi, si: (bi, 0, 0))
    return pl.pallas_call(
        functools.partial(_inproj_kernel, tm=tm),
        grid=(b, s // tm),
        in_specs=[tok(d), tok(1), const((1, d)), vec, vec, const((1, LANES)), const((1, LANES)),
                  const((1, MLA_Q_RANK)), const((1, MLA_KV_RANK)),
                  const(wa.shape), const(wuq.shape), const(wukvk.shape), const(wukvv.shape)],
        out_specs=[tok(N_HEADS * HEAD_PAD), tok(N_HEADS * HEAD_PAD),
                   pl.BlockSpec((1, N_HEADS * FOX_DIM, tm), lambda bi, si: (bi, 0, si)),
                   tok(2 * d)],
        out_shape=[jax.ShapeDtypeStruct((b, s, N_HEADS * HEAD_PAD), BF16),
                   jax.ShapeDtypeStruct((b, s, N_HEADS * HEAD_PAD), BF16),
                   jax.ShapeDtypeStruct((b, N_HEADS * FOX_DIM, s), BF16),
                   jax.ShapeDtypeStruct((b, s, 2 * d), BF16)],
        scratch_shapes=[pltpu.VMEM((1, LANES), F32)],
        compiler_params=pltpu.CompilerParams(dimension_semantics=("arbitrary", "arbitrary"),
                                             vmem_limit_bytes=VMEM_LIMIT),
        name="inproj",
    )(x, pos_f, norm_g, sc_mix, sh_mix, freq_row, bf_row, qg, kvg, wa, wuq, wukvk, wukvv)


def _attn_kernel(q_ref, k_ref, vt_ref, o_ref, st_ref, m_ref, acc_ref, *, tq, tk, nsub):
    qi = pl.program_id(2)
    assert tq == tk
    row = lax.broadcasted_iota(jnp.int32, (tk, tq), 0)
    col = lax.broadcasted_iota(jnp.int32, (tk, tq), 1)
    causal = row <= col
    ones_rows = jnp.ones((ONES_ROWS, tk), BF16)
    streams = [(hh, sub) for sub in range(nsub) for hh in range(2)]

    def scores(idx, j, slot):
        hh, sub = streams[idx]
        start = pl.multiple_of(j * tk, tk)
        q = q_ref[0, sub * tq:(sub + 1) * tq, hh * HEAD_PAD:(hh + 1) * HEAD_PAD]
        k = k_ref[0, pl.ds(start, tk), hh * HEAD_PAD:(hh + 1) * HEAD_PAD]
        st_ref[slot, idx] = lax.dot_general(k, q, (((1,), (1,)), ((), ())),
                                            preferred_element_type=F32)

    def update(idx, j, slot, masked):
        hh, _ = streams[idx]
        start = pl.multiple_of(j * tk, tk)
        st = st_ref[slot, idx]
        if masked:
            st = jnp.where(causal, st, -jnp.inf)
        m = m_ref[idx]
        m_new = jnp.maximum(m, jnp.max(st, axis=0, keepdims=True))
        m_ref[idx] = m_new
        p = jnp.exp2(st - m_new).astype(BF16)
        vt = vt_ref[0, hh * MLA_V:(hh + 1) * MLA_V, pl.ds(start, tk)]
        vt_aug = jnp.concatenate([vt, ones_rows], axis=0)
        acc_ref[idx] = (jnp.exp2(m - m_new) * acc_ref[idx]
                        + jnp.dot(vt_aug, p, preferred_element_type=F32))

    m_ref[...] = jnp.full(m_ref.shape, -jnp.inf, F32)
    acc_ref[...] = jnp.zeros(acc_ref.shape, F32)
    assert nsub % 2 == 0
    n_full = qi * nsub
    everyone = range(len(streams))
    for idx in everyone:
        scores(idx, 0, 0)

    def body(i, carry):
        j = 2 * i
        for half in range(2):
            for idx in everyone:
                scores(idx, j + half + 1, 1 - half)
            for idx in everyone:
                update(idx, j + half, half, False)
        return carry

    lax.fori_loop(0, qi * (nsub // 2), body, 0)
    for idx, (hh, sub) in enumerate(streams):
        for extra in range(sub):
            scores(idx, n_full + extra + 1, (extra + 1) % 2)
            update(idx, n_full + extra, extra % 2, False)
        update(idx, n_full + sub, sub % 2, True)
    for sub in range(nsub):
        outs = []
        for hh in range(2):
            acc = acc_ref[streams.index((hh, sub))]
            outs.append(acc[0:MLA_V, :] / acc[MLA_V:MLA_V + 1, :])
        o_ref[0, sub * tq:(sub + 1) * tq, :] = jnp.concatenate(outs, axis=0).T.astype(BF16)


def _attn_call(q_all, k_all, vt_all, tq, nsub):
    b, s, _ = q_all.shape
    pairs = N_HEADS // 2
    bq = tq * nsub
    return pl.pallas_call(
        functools.partial(_attn_kernel, tq=tq, tk=tq, nsub=nsub),
        grid=(b, pairs, s // bq),
        in_specs=[pl.BlockSpec((1, bq, 2 * HEAD_PAD), lambda bi, pi, qi: (bi, qi, pi)),
                  pl.BlockSpec((1, s, 2 * HEAD_PAD), lambda bi, pi, qi: (bi, 0, pi)),
                  pl.BlockSpec((1, LANES, s), lambda bi, pi, qi: (bi, pi, 0))],
        out_specs=pl.BlockSpec((1, bq, LANES), lambda bi, pi, qi: (bi, qi, pi)),
        out_shape=jax.ShapeDtypeStruct((b, s, pairs * LANES), BF16),
        scratch_shapes=[pltpu.VMEM((2, 2 * nsub, tq, tq), F32),
                        pltpu.VMEM((2 * nsub, 1, tq), F32),
                        pltpu.VMEM((2 * nsub, MLA_V + ONES_ROWS, tq), F32)],
        compiler_params=pltpu.CompilerParams(
            dimension_semantics=("arbitrary", "arbitrary", "arbitrary"),
            vmem_limit_bytes=VMEM_LIMIT),
        name="attn",
    )(q_all, k_all, vt_all)


def _outproj_kernel(o_ref, g_ref, x_ref, gm_ref, wm_ref, wf_ref, wo_ref, x1_ref):
    half = MLA_HEADS * MLA_V
    o = o_ref[0]
    g = g_ref[0].astype(F32)
    y = (g[:, :D_MODEL] * jnp.dot(o[:, :half], wm_ref[...], preferred_element_type=F32)
         + g[:, D_MODEL:] * jnp.dot(o[:, half:], wf_ref[...], preferred_element_type=F32))
    mix = jnp.dot(y.astype(BF16), wo_ref[...], preferred_element_type=F32)
    x1_ref[0] = x_ref[0] + gm_ref[0] * mix


def _outproj_call(o_all, g_all, x, g_m, wm, wf, wo, tm):
    b, s, d = x.shape
    const = lambda shape: pl.BlockSpec(shape, lambda bi, si: (0,) * len(shape))
    tok = lambda w: pl.BlockSpec((1, tm, w), lambda bi, si: (bi, si, 0))
    vec = pl.BlockSpec((1, 1, d), lambda bi, si: (bi, 0, 0))
    return pl.pallas_call(
        _outproj_kernel,
        grid=(b, s // tm),
        in_specs=[tok(o_all.shape[-1]), tok(g_all.shape[-1]), tok(d), vec,
                  const(wm.shape), const(wf.shape), const(wo.shape)],
        out_specs=tok(d),
        out_shape=jax.ShapeDtypeStruct((b, s, d), F32),
        compiler_params=pltpu.CompilerParams(dimension_semantics=("arbitrary", "arbitrary"),
                                             vmem_limit_bytes=VMEM_LIMIT),
        name="outproj",
    )(o_all, g_all, x, g_m, wm, wf, wo)


def _shift_rows(u, prev, k):
    tm = u.shape[0]
    body = pltpu.roll(u, k, 0)
    head = jnp.where(lax.broadcasted_iota(jnp.int32, prev.shape, 0) < k,
                     pltpu.roll(prev, k, 0), body[0:8, :])
    return jnp.concatenate([head, body[8:tm, :]], axis=0)


def _ffn_kernel(x1_ref, ng_ref, sc_ref, sh_ref, gf_ref, cw_ref, cb_ref, fg_ref, wup_ref, wdn_ref,
                out_ref, tail_ref, act_ref, *, chunk):
    s_idx = pl.program_id(1)

    @pl.when(s_idx == 0)
    def _():
        tail_ref[...] = jnp.zeros_like(tail_ref)

    x1 = x1_ref[0]
    tm = x1.shape[0]
    h = _modulated_norm(x1, ng_ref, sc_ref, sh_ref)

    def conv(c0):
        u = jnp.dot(h, wup_ref[:, c0:c0 + chunk], preferred_element_type=F32)
        prev = tail_ref[:, c0:c0 + chunk]
        tail_ref[:, c0:c0 + chunk] = u[tm - 8:tm, :]
        w = cw_ref[:, c0:c0 + chunk]
        return (cb_ref[:, c0:c0 + chunk] + w[0:1] * _shift_rows(u, prev, 2)
                + w[1:2] * _shift_rows(u, prev, 1) + w[2:3] * u)

    for c0 in range(0, D_FF, chunk):
        act_ref[:, c0:c0 + chunk] = (_silu(conv(c0)) * conv(D_FF + c0)).astype(BF16)

    y = jnp.dot(act_ref[...], wdn_ref[...], preferred_element_type=F32)
    x2 = x1 + gf_ref[0] * y
    out_ref[0] = _rms_scale(x2) * fg_ref[...]


def _ffn_call(x1, norm_g, sc_ffn, sh_ffn, g_f, conv_w, conv_b, final_g, wup, wdn, tm, chunk):
    b, s, d = x1.shape
    const = lambda shape: pl.BlockSpec(shape, lambda bi, si: (0,) * len(shape))
    tok = pl.BlockSpec((1, tm, d), lambda bi, si: (bi, si, 0))
    vec = pl.BlockSpec((1, 1, d), lambda bi, si: (bi, 0, 0))
    return pl.pallas_call(
        functools.partial(_ffn_kernel, chunk=chunk),
        grid=(b, s // tm),
        in_specs=[tok, const((1, d)), vec, vec, vec, const(conv_w.shape), const(conv_b.shape),
                  const(final_g.shape), const(wup.shape), const(wdn.shape)],
        out_specs=tok,
        out_shape=jax.ShapeDtypeStruct((b, s, d), F32),
        scratch_shapes=[pltpu.VMEM((8, 2 * D_FF), F32), pltpu.VMEM((tm, D_FF), BF16)],
        compiler_params=pltpu.CompilerParams(dimension_semantics=("arbitrary", "arbitrary"),
                                             vmem_limit_bytes=VMEM_LIMIT),
        name="ffn",
    )(x1, norm_g, sc_ffn, sh_ffn, g_f, conv_w, conv_b, final_g, wup, wdn)


def _pad_heads(w, heads, width):
    k = w.shape[0]
    w = w.reshape(k, heads, width)
    return jnp.pad(w, ((0, 0), (0, 0), (0, HEAD_PAD - width))).reshape(k, heads * HEAD_PAD)


def _prep_w_in(w_in):
    d = w_in.shape[0]
    splits = (MLA_Q_RANK, MLA_KV_RANK, MLA_ROPE, FOX_HEADS * FOX_DIM, FOX_HEADS * FOX_DIM,
              FOX_HEADS * FOX_DIM, FOX_HEADS, D_MODEL, D_MODEL)
    offs = [0]
    for n in splits:
        offs.append(offs[-1] + n)
    cq, ckv, krope, fq, fk, fv, flog, gm, gf = (w_in[:, offs[i]:offs[i + 1]] for i in range(9))
    misc = jnp.concatenate([jnp.zeros((d, MLA_NOPE), w_in.dtype), krope, flog,
                            jnp.zeros((d, LANES - FLOGIT_LO - FOX_HEADS), w_in.dtype)], axis=1)
    return jnp.concatenate([cq, ckv, misc, _pad_heads(fq, FOX_HEADS, FOX_DIM),
                            _pad_heads(fk, FOX_HEADS, FOX_DIM), fv, gm, gf], axis=1).astype(BF16)


def kernel(x, c, positions, w_ada, b_ada, norm_mix_g, w_in, q_norm_g, w_uq, kv_norm_g, w_ukv,
           b_forget, w_o_mla, w_o_fox, w_out, norm_ffn_g, w_up, conv_w, conv_b, w_down,
           norm_final_g):
    assert w_ada.shape[0] == 1, "single-layer block"
    b, s, d = x.shape
    ada = _ada_call(c, w_ada[0], b_ada[0])
    sh_m, sc_m, g_m, sh_f, sc_f, g_f = (ada[:, i * d:(i + 1) * d].reshape(b, 1, d)
                                        for i in range(N_ADA))

    inv_freq = ROPE_THETA ** (-jnp.arange(0, MLA_ROPE, 2, dtype=F32) / MLA_ROPE)
    freq_row = jnp.concatenate([jnp.zeros((MLA_NOPE,), F32), inv_freq, inv_freq,
                                jnp.zeros((LANES - FLOGIT_LO,), F32)]).reshape(1, LANES)
    bf_row = jnp.concatenate([jnp.zeros((FLOGIT_LO,), F32), b_forget[0].astype(F32),
                              jnp.zeros((LANES - FLOGIT_LO - FOX_HEADS,), F32)]).reshape(1, LANES)
    pos_f = positions.astype(F32).reshape(b, s, 1)

    wa = _prep_w_in(w_in[0])
    wuq = _pad_heads(w_uq[0], MLA_HEADS, MLA_NOPE + MLA_ROPE).astype(BF16)
    wukv = w_ukv[0].reshape(MLA_KV_RANK, MLA_HEADS, MLA_NOPE + MLA_V)
    wukvk = _pad_heads(wukv[:, :, :MLA_NOPE].reshape(MLA_KV_RANK, -1), MLA_HEADS,
                       MLA_NOPE).astype(BF16)
    wukvv = wukv[:, :, MLA_NOPE:].reshape(MLA_KV_RANK, -1).astype(BF16)

    q_all, k_all, vt_all, g_all = _inproj_call(
        x, pos_f, norm_mix_g[0].reshape(1, d), sc_m, sh_m, freq_row, bf_row,
        q_norm_g[0].reshape(1, -1),
        kv_norm_g[0].reshape(1, -1), wa, wuq, wukvk, wukvv, tm=256)
    o_all = _attn_call(q_all, k_all, vt_all, tq=256, nsub=2)
    x1 = _outproj_call(o_all, g_all, x, g_m, w_o_mla[0].astype(BF16), w_o_fox[0].astype(BF16),
                       w_out[0].astype(BF16), tm=512)
    return _ffn_call(x1, norm_ffn_g[0].reshape(1, d), sc_f, sh_f, g_f, conv_w[0],
                     conv_b[0].reshape(1, -1),
                     norm_final_g.reshape(1, -1), w_up[0].astype(BF16), w_down[0].astype(BF16),
                     tm=256, chunk=256)
```

```python
import functools
import math

import jax
import jax.numpy as jnp
from jax import lax
from jax.experimental import pallas as pl
from jax.experimental.pallas import tpu as pltpu

F32 = jnp.float32
BF16 = jnp.bfloat16

D_MODEL = 1024
MLA_HEADS = 8
MLA_Q_RANK = 384
MLA_KV_RANK = 256
MLA_NOPE = 64
MLA_ROPE = 32
MLA_V = 64
MLA_SCALE = 1.0 / math.sqrt(MLA_NOPE + MLA_ROPE)
ROPE_THETA = 10000.0
FOX_HEADS = 8
FOX_DIM = 64
FOX_SCALE = 1.0 / math.sqrt(FOX_DIM)
LOG2E = math.log2(math.e)
D_FF = 2816
CONV_WIDTH = 3
EPS = 1e-6
N_ADA = 6

LANES = 128
HEAD_PAD = LANES
N_HEADS = MLA_HEADS + FOX_HEADS
ROPE_HALF = MLA_ROPE // 2
ROPE_LO = MLA_NOPE
ROPE_HI = MLA_NOPE + ROPE_HALF
FLOGIT_LO = MLA_NOPE + MLA_ROPE
AUG_LO = FOX_DIM
ONES_ROWS = 16

VMEM_LIMIT = 56 * 1024 * 1024

_COL_CQ = 0
_COL_CKV = _COL_CQ + MLA_Q_RANK
_COL_MISC = _COL_CKV + MLA_KV_RANK
_COL_FQ = _COL_MISC + LANES
_COL_FK = _COL_FQ + FOX_HEADS * HEAD_PAD
_COL_FV = _COL_FK + FOX_HEADS * HEAD_PAD
_COL_GATE = _COL_FV + FOX_HEADS * FOX_DIM
_COL_END = _COL_GATE + 2 * D_MODEL


def _silu(x):
    return x * (1.0 / (1.0 + jnp.exp(-x)))


def _sigmoid(x):
    return 1.0 / (1.0 + jnp.exp(-x))


def _rms_scale(x):
    return x * lax.rsqrt(jnp.mean(x * x, axis=-1, keepdims=True) + EPS)


def _ada_kernel(ct_ref, w_ref, b_ref, o_ref, *, batch):
    act = _silu(ct_ref[...])
    w = w_ref[...]
    for b in range(batch):
        o_ref[b:b + 1, :] = jnp.sum(w * act[:, b:b + 1], axis=0, keepdims=True) + b_ref[...]


def _ada_call(c, w_ada, b_ada):
    batch, d = c.shape
    n = w_ada.shape[1]
    tn = 1536
    return pl.pallas_call(
        functools.partial(_ada_kernel, batch=batch),
        grid=(n // tn,),
        in_specs=[pl.BlockSpec((d, batch), lambda j: (0, 0)),
                  pl.BlockSpec((d, tn), lambda j: (0, j)),
                  pl.BlockSpec((1, tn), lambda j: (0, j))],
        out_specs=pl.BlockSpec((batch, tn), lambda j: (0, j)),
        out_shape=jax.ShapeDtypeStruct((batch, n), F32),
        compiler_params=pltpu.CompilerParams(dimension_semantics=("arbitrary",),
                                             vmem_limit_bytes=VMEM_LIMIT),
        name="ada",
    )(c.T, w_ada, b_ada.reshape(1, n))


def _rope_tables(pos_col, freq_row):
    ang = pos_col * freq_row
    lane = lax.broadcasted_iota(jnp.int32, ang.shape, 1)
    cos = jnp.cos(ang)
    sin = jnp.sin(ang)
    s_lo = jnp.where((lane >= ROPE_LO) & (lane < ROPE_HI), -sin, 0.0)
    s_hi = jnp.where((lane >= ROPE_HI) & (lane < FLOGIT_LO), sin, 0.0)
    return cos, s_lo, s_hi


def _rope_block(t, cos, s_lo, s_hi):
    return (t * cos + pltpu.roll(t, LANES - ROPE_HALF, 1) * s_lo
            + pltpu.roll(t, ROPE_HALF, 1) * s_hi)


def _split3(x):
    hi = x.astype(BF16)
    r1 = x - hi.astype(F32)
    mid = r1.astype(BF16)
    lo = (r1 - mid.astype(F32)).astype(BF16)
    return hi, mid, lo


def _modulated_norm(x, gain_ref, sc_ref, sh_ref):
    return (_rms_scale(x) * (gain_ref[...] * (1.0 + sc_ref[0])) + sh_ref[0]).astype(BF16)


def _inproj_kernel(x_ref, pos_ref, ng_ref, sc_ref, sh_ref, freq_ref, bf_ref, qg_ref, kvg_ref,
                   wa_ref, wuq_ref, wukvk_ref, wukvv_ref,
                   q_ref, k_ref, vt_ref, g_ref, carry_ref, *, tm):
    s_idx = pl.program_id(1)

    @pl.when(s_idx == 0)
    def _():
        carry_ref[...] = jnp.zeros_like(carry_ref)

    h = _modulated_norm(x_ref[0], ng_ref, sc_ref, sh_ref)

    def proj(lo, hi):
        return jnp.dot(h, wa_ref[:, lo:hi], preferred_element_type=F32)

    lane = lax.broadcasted_iota(jnp.int32, (tm, LANES), 1)
    cos, s_lo, s_hi = _rope_tables(pos_ref[0], freq_ref[...])

    cqn = (_rms_scale(proj(_COL_CQ, _COL_CKV)) * qg_ref[...]).astype(BF16)
    qm = jnp.dot(cqn, wuq_ref[...], preferred_element_type=F32)
    for hd in range(MLA_HEADS):
        blk = qm[:, hd * HEAD_PAD:(hd + 1) * HEAD_PAD]
        q_ref[0, :, hd * HEAD_PAD:(hd + 1) * HEAD_PAD] = (
            _rope_block(blk, cos, s_lo, s_hi) * (MLA_SCALE * LOG2E)).astype(BF16)

    misc = proj(_COL_MISC, _COL_FQ)
    krope = _rope_block(jnp.where(lane < FLOGIT_LO, misc, 0.0), cos, s_lo, s_hi)
    ckvn = (_rms_scale(proj(_COL_CKV, _COL_MISC)) * kvg_ref[...]).astype(BF16)
    kn = jnp.dot(ckvn, wukvk_ref[...], preferred_element_type=F32)
    for hd in range(MLA_HEADS):
        k_ref[0, :, hd * HEAD_PAD:(hd + 1) * HEAD_PAD] = (
            kn[:, hd * HEAD_PAD:(hd + 1) * HEAD_PAD] + krope).astype(BF16)
    vt_ref[0, 0:MLA_HEADS * MLA_V, :] = jnp.dot(
        ckvn, wukvv_ref[...], preferred_element_type=F32).T.astype(BF16)

    z = misc + bf_ref[...]
    logf = jnp.minimum(z, 0.0) - jnp.log1p(jnp.exp(-jnp.abs(z)))
    logf = jnp.where((lane >= FLOGIT_LO) & (lane < FLOGIT_LO + FOX_HEADS), logf, 0.0)
    row = lax.broadcasted_iota(jnp.int32, (tm, tm), 0)
    col = lax.broadcasted_iota(jnp.int32, (tm, tm), 1)
    tri = jnp.where(col <= row, 1.0, 0.0).astype(BF16)
    parts = _split3(logf)
    fcum = carry_ref[...] + sum(jnp.dot(tri, p, preferred_element_type=F32) for p in parts)
    carry_ref[...] = fcum[tm - 1:tm, :]

    fq = proj(_COL_FQ, _COL_FK)
    fk = proj(_COL_FK, _COL_FV)
    for hd in range(FOX_HEADS):
        fb = jnp.broadcast_to(fcum[:, FLOGIT_LO + hd:FLOGIT_LO + hd + 1], (tm, LANES)) * LOG2E
        hi, mid, lo = (p.astype(F32) for p in _split3(fb))
        q_aug = jnp.where(lane == AUG_LO, hi, jnp.where(lane == AUG_LO + 1, mid,
                jnp.where(lane == AUG_LO + 2, lo,
                jnp.where((lane >= AUG_LO + 3) & (lane < AUG_LO + 6), 1.0, 0.0))))
        k_aug = jnp.where(lane == AUG_LO + 3, -hi, jnp.where(lane == AUG_LO + 4, -mid,
                jnp.where(lane == AUG_LO + 5, -lo,
                jnp.where((lane >= AUG_LO) & (lane < AUG_LO + 3), 1.0, 0.0))))
        c0 = (MLA_HEADS + hd) * HEAD_PAD
        q_ref[0, :, c0:c0 + HEAD_PAD] = (
            fq[:, hd * HEAD_PAD:(hd + 1) * HEAD_PAD] * (FOX_SCALE * LOG2E) + q_aug).astype(BF16)
        k_ref[0, :, c0:c0 + HEAD_PAD] = (
            fk[:, hd * HEAD_PAD:(hd + 1) * HEAD_PAD] + k_aug).astype(BF16)
    vt_ref[0, MLA_HEADS * MLA_V:, :] = proj(_COL_FV, _COL_GATE).T.astype(BF16)

    g_ref[0] = _sigmoid(proj(_COL_GATE, _COL_END)).astype(BF16)


def _inproj_call(x, pos_f, norm_g, sc_mix, sh_mix, freq_row, bf_row, qg, kvg, wa, wuq, wukvk,
                 wukvv, tm):
    b, s, d = x.shape
    const = lambda shape: pl.BlockSpec(shape, lambda bi, si: (0,) * len(shape))
    tok = lambda w: pl.BlockSpec((1, tm, w), lambda bi, si: (bi, si, 0))
    vec = pl.BlockSpec((1, 1, d), lambda bi, si: (bi, 0, 0))
    return pl.pallas_call(
        functools.partial(_inproj_kernel, tm=tm),
        grid=(b, s // tm),
        in_specs=[tok(d), tok(1), const((1, d)), vec, vec, const((1, LANES)), const((1, LANES)),
                  const((1, MLA_Q_RANK)), const((1, MLA_KV_RANK)),
                  const(wa.shape), const(wuq.shape), const(wukvk.shape), const(wukvv.shape)],
        out_specs=[tok(N_HEADS * HEAD_PAD), tok(N_HEADS * HEAD_PAD),
                   pl.BlockSpec((1, N_HEADS * FOX_DIM, tm), lambda bi, si: (bi, 0, si)),
                   tok(2 * d)],
        out_shape=[jax.ShapeDtypeStruct((b, s, N_HEADS * HEAD_PAD), BF16),
                   jax.ShapeDtypeStruct((b, s, N_HEADS * HEAD_PAD), BF16),
                   jax.ShapeDtypeStruct((b, N_HEADS * FOX_DIM, s), BF16),
                   jax.ShapeDtypeStruct((b, s, 2 * d), BF16)],
        scratch_shapes=[pltpu.VMEM((1, LANES), F32)],
        compiler_params=pltpu.CompilerParams(dimension_semantics=("arbitrary", "arbitrary"),
                                             vmem_limit_bytes=VMEM_LIMIT),
        name="inproj",
    )(x, pos_f, norm_g, sc_mix, sh_mix, freq_row, bf_row, qg, kvg, wa, wuq, wukvk, wukvv)


def _attn_kernel(q_ref, k_ref, vt_ref, o_ref, st_ref, m_ref, acc_ref, *, tq, tk, nsub):
    qi = pl.program_id(2)
    assert tq == tk
    row = lax.broadcasted_iota(jnp.int32, (tk, tq), 0)
    col = lax.broadcasted_iota(jnp.int32, (tk, tq), 1)
    causal = row <= col
    ones_rows = jnp.ones((ONES_ROWS, tk), BF16)
    streams = [(hh, sub) for sub in range(nsub) for hh in range(2)]

    def scores(idx, j, slot):
        hh, sub = streams[idx]
        start = pl.multiple_of(j * tk, tk)
        q = q_ref[0, sub * tq:(sub + 1) * tq, hh * HEAD_PAD:(hh + 1) * HEAD_PAD]
        k = k_ref[0, pl.ds(start, tk), hh * HEAD_PAD:(hh + 1) * HEAD_PAD]
        st_ref[slot, idx] = lax.dot_general(k, q, (((1,), (1,)), ((), ())),
                                            preferred_element_type=F32)

    def update(idx, j, slot, masked):
        hh, _ = streams[idx]
        start = pl.multiple_of(j * tk, tk)
        st = st_ref[slot, idx]
        if masked:
            st = jnp.where(causal, st, -jnp.inf)
        m = m_ref[idx]
        m_new = jnp.maximum(m, jnp.max(st, axis=0, keepdims=True))
        m_ref[idx] = m_new
        p = jnp.exp2(st - m_new).astype(BF16)
        vt = vt_ref[0, hh * MLA_V:(hh + 1) * MLA_V, pl.ds(start, tk)]
        vt_aug = jnp.concatenate([vt, ones_rows], axis=0)
        acc_ref[idx] = (jnp.exp2(m - m_new) * acc_ref[idx]
                        + jnp.dot(vt_aug, p, preferred_element_type=F32))

    def step(j, slot, diag_sub=None):
        first = 0 if diag_sub is None else diag_sub
        for idx, (_, sub) in enumerate(streams):
            if sub > first or diag_sub is None:
                scores(idx, j + 1, 1 - slot)
        for idx, (_, sub) in enumerate(streams):
            if sub >= first:
                update(idx, j, slot, masked=(sub == diag_sub))

    m_ref[...] = jnp.full(m_ref.shape, -jnp.inf, F32)
    acc_ref[...] = jnp.zeros(acc_ref.shape, F32)
    assert nsub % 2 == 0
    n_full = qi * nsub
    for idx in range(len(streams)):
        scores(idx, 0, 0)

    def body(i, carry):
        for u in range(nsub):
            step(nsub * i + u, u % 2)
        return carry

    lax.fori_loop(0, qi, body, 0)
    for r in range(nsub):
        step(n_full + r, r % 2, diag_sub=r)


    for sub in range(nsub):
        outs = []
        for hh in range(2):
            acc = acc_ref[streams.index((hh, sub))]
            outs.append(acc[0:MLA_V, :] / acc[MLA_V:MLA_V + 1, :])
        o_ref[0, sub * tq:(sub + 1) * tq, :] = jnp.concatenate(outs, axis=0).T.astype(BF16)


def _attn_call(q_all, k_all, vt_all, tq, nsub):
    b, s, _ = q_all.shape
    pairs = N_HEADS // 2
    bq = tq * nsub
    return pl.pallas_call(
        functools.partial(_attn_kernel, tq=tq, tk=tq, nsub=nsub),
        grid=(b, pairs, s // bq),
        in_specs=[pl.BlockSpec((1, bq, 2 * HEAD_PAD), lambda bi, pi, qi: (bi, qi, pi)),
                  pl.BlockSpec((1, s, 2 * HEAD_PAD), lambda bi, pi, qi: (bi, 0, pi)),
                  pl.BlockSpec((1, LANES, s), lambda bi, pi, qi: (bi, pi, 0))],
        out_specs=pl.BlockSpec((1, bq, LANES), lambda bi, pi, qi: (bi, qi, pi)),
        out_shape=jax.ShapeDtypeStruct((b, s, pairs * LANES), BF16),
        scratch_shapes=[pltpu.VMEM((2, 2 * nsub, tq, tq), F32),
                        pltpu.VMEM((2 * nsub, 1, tq), F32),
                        pltpu.VMEM((2 * nsub, MLA_V + ONES_ROWS, tq), F32)],
        compiler_params=pltpu.CompilerParams(
            dimension_semantics=("arbitrary", "arbitrary", "arbitrary"),
            vmem_limit_bytes=VMEM_LIMIT),
        name="attn",
    )(q_all, k_all, vt_all)


def _outproj_kernel(o_ref, g_ref, x_ref, gm_ref, wm_ref, wf_ref, wo_ref, x1_ref):
    half = MLA_HEADS * MLA_V
    o = o_ref[0]
    g = g_ref[0].astype(F32)
    y = (g[:, :D_MODEL] * jnp.dot(o[:, :half], wm_ref[...], preferred_element_type=F32)
         + g[:, D_MODEL:] * jnp.dot(o[:, half:], wf_ref[...], preferred_element_type=F32))
    mix = jnp.dot(y.astype(BF16), wo_ref[...], preferred_element_type=F32)
    x1_ref[0] = x_ref[0] + gm_ref[0] * mix


def _outproj_call(o_all, g_all, x, g_m, wm, wf, wo, tm):
    b, s, d = x.shape
    const = lambda shape: pl.BlockSpec(shape, lambda bi, si: (0,) * len(shape))
    tok = lambda w: pl.BlockSpec((1, tm, w), lambda bi, si: (bi, si, 0))
    vec = pl.BlockSpec((1, 1, d), lambda bi, si: (bi, 0, 0))
    return pl.pallas_call(
        _outproj_kernel,
        grid=(b, s // tm),
        in_specs=[tok(o_all.shape[-1]), tok(g_all.shape[-1]), tok(d), vec,
                  const(wm.shape), const(wf.shape), const(wo.shape)],
        out_specs=tok(d),
        out_shape=jax.ShapeDtypeStruct((b, s, d), F32),
        compiler_params=pltpu.CompilerParams(dimension_semantics=("arbitrary", "arbitrary"),
                                             vmem_limit_bytes=VMEM_LIMIT),
        name="outproj",
    )(o_all, g_all, x, g_m, wm, wf, wo)


def _shift_rows(u, prev, k):
    tm = u.shape[0]
    body = pltpu.roll(u, k, 0)
    head = jnp.where(lax.broadcasted_iota(jnp.int32, prev.shape, 0) < k,
                     pltpu.roll(prev, k, 0), body[0:8, :])
    return jnp.concatenate([head, body[8:tm, :]], axis=0)


def _ffn_kernel(x1_ref, ng_ref, sc_ref, sh_ref, gf_ref, cw_ref, cb_ref, fg_ref, wup_ref, wdn_ref,
                out_ref, tail_ref, act_ref, *, chunk):
    s_idx = pl.program_id(1)

    @pl.when(s_idx == 0)
    def _():
        tail_ref[...] = jnp.zeros_like(tail_ref)

    x1 = x1_ref[0]
    tm = x1.shape[0]
    h = _modulated_norm(x1, ng_ref, sc_ref, sh_ref)

    def conv(c0):
        u = jnp.dot(h, wup_ref[:, c0:c0 + chunk], preferred_element_type=F32)
        prev = tail_ref[:, c0:c0 + chunk]
        tail_ref[:, c0:c0 + chunk] = u[tm - 8:tm, :]
        w = cw_ref[:, c0:c0 + chunk]
        return (cb_ref[:, c0:c0 + chunk] + w[0:1] * _shift_rows(u, prev, 2)
                + w[1:2] * _shift_rows(u, prev, 1) + w[2:3] * u)

    for c0 in range(0, D_FF, chunk):
        act_ref[:, c0:c0 + chunk] = (_silu(conv(c0)) * conv(D_FF + c0)).astype(BF16)

    y = jnp.dot(act_ref[...], wdn_ref[...], preferred_element_type=F32)
    x2 = x1 + gf_ref[0] * y
    out_ref[0] = _rms_scale(x2) * fg_ref[...]


def _ffn_call(x1, norm_g, sc_ffn, sh_ffn, g_f, conv_w, conv_b, final_g, wup, wdn, tm, chunk):
    b, s, d = x1.shape
    const = lambda shape: pl.BlockSpec(shape, lambda bi, si: (0,) * len(shape))
    tok = pl.BlockSpec((1, tm, d), lambda bi, si: (bi, si, 0))
    vec = pl.BlockSpec((1, 1, d), lambda bi, si: (bi, 0, 0))
    return pl.pallas_call(
        functools.partial(_ffn_kernel, chunk=chunk),
        grid=(b, s // tm),
        in_specs=[tok, const((1, d)), vec, vec, vec, const(conv_w.shape), const(conv_b.shape),
                  const(final_g.shape), const(wup.shape), const(wdn.shape)],
        out_specs=tok,
        out_shape=jax.ShapeDtypeStruct((b, s, d), F32),
        scratch_shapes=[pltpu.VMEM((8, 2 * D_FF), F32), pltpu.VMEM((tm, D_FF), BF16)],
        compiler_params=pltpu.CompilerParams(dimension_semantics=("arbitrary", "arbitrary"),
                                             vmem_limit_bytes=VMEM_LIMIT),
        name="ffn",
    )(x1, norm_g, sc_ffn, sh_ffn, g_f, conv_w, conv_b, final_g, wup, wdn)


def _pad_heads(w, heads, width):
    k = w.shape[0]
    w = w.reshape(k, heads, width)
    return jnp.pad(w, ((0, 0), (0, 0), (0, HEAD_PAD - width))).reshape(k, heads * HEAD_PAD)


def _prep_w_in(w_in):
    d = w_in.shape[0]
    splits = (MLA_Q_RANK, MLA_KV_RANK, MLA_ROPE, FOX_HEADS * FOX_DIM, FOX_HEADS * FOX_DIM,
              FOX_HEADS * FOX_DIM, FOX_HEADS, D_MODEL, D_MODEL)
    offs = [0]
    for n in splits:
        offs.append(offs[-1] + n)
    cq, ckv, krope, fq, fk, fv, flog, gm, gf = (w_in[:, offs[i]:offs[i + 1]] for i in range(9))
    misc = jnp.concatenate([jnp.zeros((d, MLA_NOPE), w_in.dtype), krope, flog,
                            jnp.zeros((d, LANES - FLOGIT_LO - FOX_HEADS), w_in.dtype)], axis=1)
    return jnp.concatenate([cq, ckv, misc, _pad_heads(fq, FOX_HEADS, FOX_DIM),
                            _pad_heads(fk, FOX_HEADS, FOX_DIM), fv, gm, gf], axis=1).astype(BF16)


def kernel(x, c, positions, w_ada, b_ada, norm_mix_g, w_in, q_norm_g, w_uq, kv_norm_g, w_ukv,
           b_forget, w_o_mla, w_o_fox, w_out, norm_ffn_g, w_up, conv_w, conv_b, w_down,
           norm_final_g):
    assert w_ada.shape[0] == 1, "single-layer block"
    b, s, d = x.shape
    ada = _ada_call(c, w_ada[0], b_ada[0])
    sh_m, sc_m, g_m, sh_f, sc_f, g_f = (ada[:, i * d:(i + 1) * d].reshape(b, 1, d)
                                        for i in range(N_ADA))

    inv_freq = ROPE_THETA ** (-jnp.arange(0, MLA_ROPE, 2, dtype=F32) / MLA_ROPE)
    freq_row = jnp.concatenate([jnp.zeros((MLA_NOPE,), F32), inv_freq, inv_freq,
                                jnp.zeros((LANES - FLOGIT_LO,), F32)]).reshape(1, LANES)
    bf_row = jnp.concatenate([jnp.zeros((FLOGIT_LO,), F32), b_forget[0].astype(F32),
                              jnp.zeros((LANES - FLOGIT_LO - FOX_HEADS,), F32)]).reshape(1, LANES)
    pos_f = positions.astype(F32).reshape(b, s, 1)

    wa = _prep_w_in(w_in[0])
    wuq = _pad_heads(w_uq[0], MLA_HEADS, MLA_NOPE + MLA_ROPE).astype(BF16)
    wukv = w_ukv[0].reshape(MLA_KV_RANK, MLA_HEADS, MLA_NOPE + MLA_V)
    wukvk = _pad_heads(wukv[:, :, :MLA_NOPE].reshape(MLA_KV_RANK, -1), MLA_HEADS,
                       MLA_NOPE).astype(BF16)
    wukvv = wukv[:, :, MLA_NOPE:].reshape(MLA_KV_RANK, -1).astype(BF16)

    q_all, k_all, vt_all, g_all = _inproj_call(
        x, pos_f, norm_mix_g[0].reshape(1, d), sc_m, sh_m, freq_row, bf_row,
        q_norm_g[0].reshape(1, -1),
        kv_norm_g[0].reshape(1, -1), wa, wuq, wukvk, wukvv, tm=256)
    o_all = _attn_call(q_all, k_all, vt_all, tq=256, nsub=4)
    x1 = _outproj_call(o_all, g_all, x, g_m, w_o_mla[0].astype(BF16), w_o_fox[0].astype(BF16),
                       w_out[0].astype(BF16), tm=512)
    return _ffn_call(x1, norm_ffn_g[0].reshape(1, d), sc_f, sh_f, g_f, conv_w[0],
                     conv_b[0].reshape(1, -1),
                     norm_final_g.reshape(1, -1), w_up[0].astype(BF16), w_down[0].astype(BF16),
                     tm=256, chunk=256)
```

```python
import functools
import math

import jax
import jax.numpy as jnp
from jax import lax
from jax.experimental import pallas as pl
from jax.experimental.pallas import tpu as pltpu

F32 = jnp.float32
BF16 = jnp.bfloat16

D_MODEL = 1024
MLA_HEADS = 8
MLA_Q_RANK = 384
MLA_KV_RANK = 256
MLA_NOPE = 64
MLA_ROPE = 32
MLA_V = 64
MLA_SCALE = 1.0 / math.sqrt(MLA_NOPE + MLA_ROPE)
ROPE_THETA = 10000.0
FOX_HEADS = 8
FOX_DIM = 64
FOX_SCALE = 1.0 / math.sqrt(FOX_DIM)
LOG2E = math.log2(math.e)
D_FF = 2816
CONV_WIDTH = 3
EPS = 1e-6
N_ADA = 6

LANES = 128
HEAD_PAD = LANES
N_HEADS = MLA_HEADS + FOX_HEADS
ROPE_HALF = MLA_ROPE // 2
ROPE_LO = MLA_NOPE
ROPE_HI = MLA_NOPE + ROPE_HALF
FLOGIT_LO = MLA_NOPE + MLA_ROPE
AUG_LO = FOX_DIM
ONES_ROWS = 16

VMEM_LIMIT = 56 * 1024 * 1024


def _silu(x):
    return x * (1.0 / (1.0 + jnp.exp(-x)))


def _sigmoid(x):
    return 1.0 / (1.0 + jnp.exp(-x))


def _rms_scale(x):
    return x * lax.rsqrt(jnp.mean(x * x, axis=-1, keepdims=True) + EPS)


def _ada_kernel(ct_ref, w_ref, b_ref, o_ref, *, batch):
    act = _silu(ct_ref[...])
    w = w_ref[...]
    for b in range(batch):
        o_ref[b:b + 1, :] = jnp.sum(w * act[:, b:b + 1], axis=0, keepdims=True) + b_ref[...]


def _ada_call(c, w_ada, b_ada):
    batch, d = c.shape
    n = w_ada.shape[1]
    tn = 1536
    return pl.pallas_call(
        functools.partial(_ada_kernel, batch=batch),
        grid=(n // tn,),
        in_specs=[pl.BlockSpec((d, batch), lambda j: (0, 0)),
                  pl.BlockSpec((d, tn), lambda j: (0, j)),
                  pl.BlockSpec((1, tn), lambda j: (0, j))],
        out_specs=pl.BlockSpec((batch, tn), lambda j: (0, j)),
        out_shape=jax.ShapeDtypeStruct((batch, n), F32),
        compiler_params=pltpu.CompilerParams(dimension_semantics=("arbitrary",),
                                             vmem_limit_bytes=VMEM_LIMIT),
        name="ada",
    )(c.T, w_ada, b_ada.reshape(1, n))


def _rope_tables(pos_col, freq_row):
    ang = pos_col * freq_row
    lane = lax.broadcasted_iota(jnp.int32, ang.shape, 1)
    cos = jnp.cos(ang)
    sin = jnp.sin(ang)
    s_lo = jnp.where((lane >= ROPE_LO) & (lane < ROPE_HI), -sin, 0.0)
    s_hi = jnp.where((lane >= ROPE_HI) & (lane < FLOGIT_LO), sin, 0.0)
    return cos, s_lo, s_hi


def _rope_block(t, cos, s_lo, s_hi):
    return (t * cos + pltpu.roll(t, LANES - ROPE_HALF, 1) * s_lo
            + pltpu.roll(t, ROPE_HALF, 1) * s_hi)


def _split3(x):
    hi = x.astype(BF16)
    r1 = x - hi.astype(F32)
    mid = r1.astype(BF16)
    lo = (r1 - mid.astype(F32)).astype(BF16)
    return hi, mid, lo


def _modulated_norm(x, gain_ref, sc_ref, sh_ref):
    return (_rms_scale(x) * (gain_ref[...] * (1.0 + sc_ref[0])) + sh_ref[0]).astype(BF16)


def _inproj_kernel(x_ref, pos_ref, ng_ref, sc_ref, sh_ref, freq_ref, bf_ref, qg_ref, kvg_ref,
                   wlat_ref, wmisc_ref, wfq_ref, wfk_ref, wfv_ref, wgate_ref,
                   wuq_ref, wukvk_ref, wukvv_ref, sq_ref, sk_ref, qc_ref, kc_ref,
                   qt_ref, k_ref, vt_ref, g_ref, carry_ref, *, tm):
    s_idx = pl.program_id(1)

    @pl.when(s_idx == 0)
    def _():
        carry_ref[...] = jnp.zeros_like(carry_ref)

    h = _modulated_norm(x_ref[0], ng_ref, sc_ref, sh_ref)

    def proj(w_ref, lo=None, hi=None):
        w = w_ref[...] if lo is None else w_ref[:, lo:hi]
        return jnp.dot(h, w, preferred_element_type=F32)

    lane = lax.broadcasted_iota(jnp.int32, (tm, LANES), 1)
    cos, s_lo, s_hi = _rope_tables(pos_ref[0], freq_ref[...])

    cqn = (_rms_scale(proj(wlat_ref, 0, MLA_Q_RANK)) * qg_ref[...]).astype(BF16)
    qm = jnp.dot(cqn, wuq_ref[...], preferred_element_type=F32)
    for hd in range(MLA_HEADS):
        blk = qm[:, hd * HEAD_PAD:(hd + 1) * HEAD_PAD]
        qt_ref[0, hd * HEAD_PAD:(hd + 1) * HEAD_PAD, :] = (
            _rope_block(blk, cos, s_lo, s_hi) * (MLA_SCALE * LOG2E)).T.astype(BF16)

    misc = proj(wmisc_ref)
    krope = _rope_block(jnp.where(lane < FLOGIT_LO, misc, 0.0), cos, s_lo, s_hi)
    ckvn = (_rms_scale(proj(wlat_ref, MLA_Q_RANK, MLA_Q_RANK + MLA_KV_RANK))
            * kvg_ref[...]).astype(BF16)
    kn = jnp.dot(ckvn, wukvk_ref[...], preferred_element_type=F32)
    for hd in range(MLA_HEADS):
        k_ref[0, :, hd * HEAD_PAD:(hd + 1) * HEAD_PAD] = (
            kn[:, hd * HEAD_PAD:(hd + 1) * HEAD_PAD] + krope).astype(BF16)
    vt_ref[0, 0:MLA_HEADS * MLA_V, :] = jnp.dot(
        ckvn, wukvv_ref[...], preferred_element_type=F32).T.astype(BF16)

    z = misc + bf_ref[...]
    logf = jnp.minimum(z, 0.0) - jnp.log1p(jnp.exp(-jnp.abs(z)))
    logf = jnp.where((lane >= FLOGIT_LO) & (lane < FLOGIT_LO + FOX_HEADS), logf, 0.0)
    row = lax.broadcasted_iota(jnp.int32, (tm, tm), 0)
    col = lax.broadcasted_iota(jnp.int32, (tm, tm), 1)
    tri = jnp.where(col <= row, 1.0, 0.0).astype(BF16)
    fcum = carry_ref[...] + sum(jnp.dot(tri, p, preferred_element_type=F32)
                                for p in _split3(logf))
    carry_ref[...] = fcum[tm - 1:tm, :]

    hi, mid, lo = (p.astype(F32) for p in _split3(fcum * LOG2E))
    bias3 = (hi + pltpu.roll(mid, FOX_HEADS, 1) + pltpu.roll(lo, 2 * FOX_HEADS, 1)).astype(BF16)
    qt_ref[0, MLA_HEADS * HEAD_PAD:, :] = (
        proj(wfq_ref) * (FOX_SCALE * LOG2E)
        + jnp.dot(bias3, sq_ref[...], preferred_element_type=F32) + qc_ref[...]).T.astype(BF16)
    k_ref[0, :, MLA_HEADS * HEAD_PAD:] = (
        proj(wfk_ref)
        + jnp.dot(bias3, sk_ref[...], preferred_element_type=F32) + kc_ref[...]).astype(BF16)
    vt_ref[0, MLA_HEADS * MLA_V:, :] = proj(wfv_ref).T.astype(BF16)

    g_ref[0] = _sigmoid(proj(wgate_ref)).astype(BF16)


def _resident(shape):
    return pl.BlockSpec(shape, lambda bi, si: (0,) * len(shape), pipeline_mode=pl.Buffered(1))


def _inproj_call(x, pos_f, norm_g, sc_mix, sh_mix, freq_row, bf_row, qg, kvg, weights, tm):
    b, s, d = x.shape
    tok = lambda w: pl.BlockSpec((1, tm, w), lambda bi, si: (bi, si, 0))
    vec = pl.BlockSpec((1, 1, d), lambda bi, si: (bi, 0, 0))
    return pl.pallas_call(
        functools.partial(_inproj_kernel, tm=tm),
        grid=(b, s // tm),
        in_specs=[tok(d), tok(1), _resident((1, d)), vec, vec, _resident((1, LANES)),
                  _resident((1, LANES)), _resident((1, MLA_Q_RANK)), _resident((1, MLA_KV_RANK))]
                 + [_resident(w.shape) for w in weights],
        out_specs=[pl.BlockSpec((1, N_HEADS * HEAD_PAD, tm), lambda bi, si: (bi, 0, si)),
                   tok(N_HEADS * HEAD_PAD),
                   pl.BlockSpec((1, N_HEADS * FOX_DIM, tm), lambda bi, si: (bi, 0, si)),
                   tok(2 * d)],
        out_shape=[jax.ShapeDtypeStruct((b, N_HEADS * HEAD_PAD, s), BF16),
                   jax.ShapeDtypeStruct((b, s, N_HEADS * HEAD_PAD), BF16),
                   jax.ShapeDtypeStruct((b, N_HEADS * FOX_DIM, s), BF16),
                   jax.ShapeDtypeStruct((b, s, 2 * d), BF16)],
        scratch_shapes=[pltpu.VMEM((1, LANES), F32)],
        compiler_params=pltpu.CompilerParams(dimension_semantics=("arbitrary", "arbitrary"),
                                             vmem_limit_bytes=VMEM_LIMIT),
        name="inproj",
    )(x, pos_f, norm_g, sc_mix, sh_mix, freq_row, bf_row, qg, kvg, *weights)


def _attn_kernel(qt_ref, k_ref, vt_ref, o_ref, st_ref, m_ref, acc_ref, *, tq, tk, nsub):
    qi = pl.program_id(2)
    assert tq == tk
    row = lax.broadcasted_iota(jnp.int32, (tk, tq), 0)
    col = lax.broadcasted_iota(jnp.int32, (tk, tq), 1)
    causal = row <= col
    ones_rows = jnp.ones((ONES_ROWS, tk), BF16)
    streams = [(hh, sub) for sub in range(nsub) for hh in range(2)]

    def scores(idx, j, slot):
        hh, sub = streams[idx]
        start = pl.multiple_of(j * tk, tk)
        qt = qt_ref[0, hh * HEAD_PAD:(hh + 1) * HEAD_PAD, sub * tq:(sub + 1) * tq]
        k = k_ref[0, pl.ds(start, tk), hh * HEAD_PAD:(hh + 1) * HEAD_PAD]
        st_ref[slot, idx] = jnp.dot(k, qt, preferred_element_type=F32)

    def update(idx, j, slot, masked):
        hh, _ = streams[idx]
        start = pl.multiple_of(j * tk, tk)
        st = st_ref[slot, idx]
        if masked:
            st = jnp.where(causal, st, -jnp.inf)
        m = m_ref[idx]
        m_new = jnp.maximum(m, jnp.max(st, axis=0, keepdims=True))
        m_ref[idx] = m_new
        p = jnp.exp2(st - m_new).astype(BF16)
        vt = vt_ref[0, hh * MLA_V:(hh + 1) * MLA_V, pl.ds(start, tk)]
        vt_aug = jnp.concatenate([vt, ones_rows], axis=0)
        acc_ref[idx] = (jnp.exp2(m - m_new) * acc_ref[idx]
                        + jnp.dot(vt_aug, p, preferred_element_type=F32))

    def step(j, slot, diag_sub=None):
        first = 0 if diag_sub is None else diag_sub
        for idx, (_, sub) in enumerate(streams):
            if sub > first or diag_sub is None:
                scores(idx, j + 1, 1 - slot)
        for idx, (_, sub) in enumerate(streams):
            if sub >= first:
                update(idx, j, slot, masked=(sub == diag_sub))

    m_ref[...] = jnp.full(m_ref.shape, -jnp.inf, F32)
    acc_ref[...] = jnp.zeros(acc_ref.shape, F32)
    assert nsub % 2 == 0
    n_full = qi * nsub
    for idx in range(len(streams)):
        scores(idx, 0, 0)

    def body(i, carry):
        for u in range(nsub):
            step(nsub * i + u, u % 2)
        return carry

    lax.fori_loop(0, qi, body, 0)
    for r in range(nsub):
        step(n_full + r, r % 2, diag_sub=r)


    for sub in range(nsub):
        outs = []
        for hh in range(2):
            acc = acc_ref[streams.index((hh, sub))]
            outs.append(acc[0:MLA_V, :] / acc[MLA_V:MLA_V + 1, :])
        o_ref[0, sub * tq:(sub + 1) * tq, :] = jnp.concatenate(outs, axis=0).T.astype(BF16)


def _attn_call(qt_all, k_all, vt_all, tq, nsub):
    b, s, _ = k_all.shape
    pairs = N_HEADS // 2
    bq = tq * nsub
    return pl.pallas_call(
        functools.partial(_attn_kernel, tq=tq, tk=tq, nsub=nsub),
        grid=(b, pairs, s // bq),
        in_specs=[pl.BlockSpec((1, 2 * HEAD_PAD, bq), lambda bi, pi, qi: (bi, pi, qi)),
                  pl.BlockSpec((1, s, 2 * HEAD_PAD), lambda bi, pi, qi: (bi, 0, pi)),
                  pl.BlockSpec((1, LANES, s), lambda bi, pi, qi: (bi, pi, 0))],
        out_specs=pl.BlockSpec((1, bq, LANES), lambda bi, pi, qi: (bi, qi, pi)),
        out_shape=jax.ShapeDtypeStruct((b, s, pairs * LANES), BF16),
        scratch_shapes=[pltpu.VMEM((2, 2 * nsub, tq, tq), F32),
                        pltpu.VMEM((2 * nsub, 1, tq), F32),
                        pltpu.VMEM((2 * nsub, MLA_V + ONES_ROWS, tq), F32)],
        compiler_params=pltpu.CompilerParams(
            dimension_semantics=("arbitrary", "arbitrary", "arbitrary"),
            vmem_limit_bytes=VMEM_LIMIT),
        name="attn",
    )(qt_all, k_all, vt_all)


def _outproj_kernel(o_ref, g_ref, x_ref, gm_ref, wm_ref, wf_ref, wo_ref, x1_ref):
    half = MLA_HEADS * MLA_V
    o = o_ref[0]
    g = g_ref[0].astype(F32)
    y = (g[:, :D_MODEL] * jnp.dot(o[:, :half], wm_ref[...], preferred_element_type=F32)
         + g[:, D_MODEL:] * jnp.dot(o[:, half:], wf_ref[...], preferred_element_type=F32))
    mix = jnp.dot(y.astype(BF16), wo_ref[...], preferred_element_type=F32)
    x1_ref[0] = x_ref[0] + gm_ref[0] * mix


def _outproj_call(o_all, g_all, x, g_m, wm, wf, wo, tm):
    b, s, d = x.shape
    const = _resident
    tok = lambda w: pl.BlockSpec((1, tm, w), lambda bi, si: (bi, si, 0))
    vec = pl.BlockSpec((1, 1, d), lambda bi, si: (bi, 0, 0))
    return pl.pallas_call(
        _outproj_kernel,
        grid=(b, s // tm),
        in_specs=[tok(o_all.shape[-1]), tok(g_all.shape[-1]), tok(d), vec,
                  const(wm.shape), const(wf.shape), const(wo.shape)],
        out_specs=tok(d),
        out_shape=jax.ShapeDtypeStruct((b, s, d), F32),
        compiler_params=pltpu.CompilerParams(dimension_semantics=("arbitrary", "arbitrary"),
                                             vmem_limit_bytes=VMEM_LIMIT),
        name="outproj",
    )(o_all, g_all, x, g_m, wm, wf, wo)


def _shift_rows(u, prev, k):
    tm = u.shape[0]
    body = pltpu.roll(u, k, 0)
    head = jnp.where(lax.broadcasted_iota(jnp.int32, prev.shape, 0) < k,
                     pltpu.roll(prev, k, 0), body[0:8, :])
    return jnp.concatenate([head, body[8:tm, :]], axis=0)


def _ffn_kernel(x1_ref, ng_ref, sc_ref, sh_ref, gf_ref, cw_ref, cb_ref, fg_ref, wup_ref, wdn_ref,
                out_ref, tail_ref, act_ref, *, chunk):
    s_idx = pl.program_id(1)

    @pl.when(s_idx == 0)
    def _():
        tail_ref[...] = jnp.zeros_like(tail_ref)

    x1 = x1_ref[0]
    tm = x1.shape[0]
    h = _modulated_norm(x1, ng_ref, sc_ref, sh_ref)

    def conv(c0):
        u = jnp.dot(h, wup_ref[:, c0:c0 + chunk], preferred_element_type=F32)
        prev = tail_ref[:, c0:c0 + chunk]
        tail_ref[:, c0:c0 + chunk] = u[tm - 8:tm, :]
        w = cw_ref[:, c0:c0 + chunk]
        return (cb_ref[:, c0:c0 + chunk] + w[0:1] * _shift_rows(u, prev, 2)
                + w[1:2] * _shift_rows(u, prev, 1) + w[2:3] * u)

    for c0 in range(0, D_FF, chunk):
        act_ref[:, c0:c0 + chunk] = (_silu(conv(c0)) * conv(D_FF + c0)).astype(BF16)

    y = jnp.dot(act_ref[...], wdn_ref[...], preferred_element_type=F32)
    x2 = x1 + gf_ref[0] * y
    out_ref[0] = _rms_scale(x2) * fg_ref[...]


def _ffn_call(x1, norm_g, sc_ffn, sh_ffn, g_f, conv_w, conv_b, final_g, wup, wdn, tm, chunk):
    b, s, d = x1.shape
    const = _resident
    tok = pl.BlockSpec((1, tm, d), lambda bi, si: (bi, si, 0))
    vec = pl.BlockSpec((1, 1, d), lambda bi, si: (bi, 0, 0))
    return pl.pallas_call(
        functools.partial(_ffn_kernel, chunk=chunk),
        grid=(b, s // tm),
        in_specs=[tok, const((1, d)), vec, vec, vec, const(conv_w.shape), const(conv_b.shape),
                  const(final_g.shape), const(wup.shape), const(wdn.shape)],
        out_specs=tok,
        out_shape=jax.ShapeDtypeStruct((b, s, d), F32),
        scratch_shapes=[pltpu.VMEM((8, 2 * D_FF), F32), pltpu.VMEM((tm, D_FF), BF16)],
        compiler_params=pltpu.CompilerParams(dimension_semantics=("arbitrary", "arbitrary"),
                                             vmem_limit_bytes=VMEM_LIMIT),
        name="ffn",
    )(x1, norm_g, sc_ffn, sh_ffn, g_f, conv_w, conv_b, final_g, wup, wdn)


def _pad_heads(w, heads, width):
    k = w.shape[0]
    w = w.reshape(k, heads, width)
    return jnp.pad(w, ((0, 0), (0, 0), (0, HEAD_PAD - width))).reshape(k, heads * HEAD_PAD)


def _prep_w_in(w_in):
    d = w_in.shape[0]
    splits = (MLA_Q_RANK, MLA_KV_RANK, MLA_ROPE, FOX_HEADS * FOX_DIM, FOX_HEADS * FOX_DIM,
              FOX_HEADS * FOX_DIM, FOX_HEADS, D_MODEL, D_MODEL)
    offs = [0]
    for n in splits:
        offs.append(offs[-1] + n)
    piece = lambda i, j=None: w_in[:, offs[i]:offs[i + 1 if j is None else j]]
    misc = jnp.concatenate([jnp.zeros((d, MLA_NOPE), w_in.dtype), piece(2), piece(6),
                            jnp.zeros((d, LANES - FLOGIT_LO - FOX_HEADS), w_in.dtype)], axis=1)
    return [piece(0, 2).astype(BF16), misc.astype(BF16),
            _pad_heads(piece(3), FOX_HEADS, FOX_DIM).astype(BF16),
            _pad_heads(piece(4), FOX_HEADS, FOX_DIM).astype(BF16),
            piece(5).astype(BF16), piece(7, 9).astype(BF16)]


def _bias_placement():
    part, head = jnp.meshgrid(jnp.arange(3), jnp.arange(FOX_HEADS), indexing="ij")
    rows = (FLOGIT_LO + part * FOX_HEADS + head).reshape(-1)
    q_cols = (head * HEAD_PAD + AUG_LO + part).reshape(-1)
    k_cols = (head * HEAD_PAD + AUG_LO + 3 + part).reshape(-1)
    width = FOX_HEADS * HEAD_PAD
    sq = jnp.zeros((LANES, width), F32).at[rows, q_cols].set(1.0)
    sk = jnp.zeros((LANES, width), F32).at[rows, k_cols].set(-1.0)
    qc = jnp.zeros((1, width), F32).at[0, k_cols].set(1.0)
    kc = jnp.zeros((1, width), F32).at[0, q_cols].set(1.0)
    return sq.astype(BF16), sk.astype(BF16), qc, kc


def kernel(x, c, positions, w_ada, b_ada, norm_mix_g, w_in, q_norm_g, w_uq, kv_norm_g, w_ukv,
           b_forget, w_o_mla, w_o_fox, w_out, norm_ffn_g, w_up, conv_w, conv_b, w_down,
           norm_final_g):
    assert w_ada.shape[0] == 1, "single-layer block"
    b, s, d = x.shape
    ada = _ada_call(c, w_ada[0], b_ada[0])
    sh_m, sc_m, g_m, sh_f, sc_f, g_f = (ada[:, i * d:(i + 1) * d].reshape(b, 1, d)
                                        for i in range(N_ADA))

    inv_freq = ROPE_THETA ** (-jnp.arange(0, MLA_ROPE, 2, dtype=F32) / MLA_ROPE)
    freq_row = jnp.concatenate([jnp.zeros((MLA_NOPE,), F32), inv_freq, inv_freq,
                                jnp.zeros((LANES - FLOGIT_LO,), F32)]).reshape(1, LANES)
    bf_row = jnp.concatenate([jnp.zeros((FLOGIT_LO,), F32), b_forget[0].astype(F32),
                              jnp.zeros((LANES - FLOGIT_LO - FOX_HEADS,), F32)]).reshape(1, LANES)
    pos_f = positions.astype(F32).reshape(b, s, 1)

    wuq = _pad_heads(w_uq[0], MLA_HEADS, MLA_NOPE + MLA_ROPE).astype(BF16)
    wukv = w_ukv[0].reshape(MLA_KV_RANK, MLA_HEADS, MLA_NOPE + MLA_V)
    wukvk = _pad_heads(wukv[:, :, :MLA_NOPE].reshape(MLA_KV_RANK, -1), MLA_HEADS,
                       MLA_NOPE).astype(BF16)
    wukvv = wukv[:, :, MLA_NOPE:].reshape(MLA_KV_RANK, -1).astype(BF16)

    qt_all, k_all, vt_all, g_all = _inproj_call(
        x, pos_f, norm_mix_g[0].reshape(1, d), sc_m, sh_m, freq_row, bf_row,
        q_norm_g[0].reshape(1, -1), kv_norm_g[0].reshape(1, -1),
        _prep_w_in(w_in[0]) + [wuq, wukvk, wukvv] + list(_bias_placement()), tm=512)
    o_all = _attn_call(qt_all, k_all, vt_all, tq=256, nsub=4)
    x1 = _outproj_call(o_all, g_all, x, g_m, w_o_mla[0].astype(BF16), w_o_fox[0].astype(BF16),
                       w_out[0].astype(BF16), tm=512)
    return _ffn_call(x1, norm_ffn_g[0].reshape(1, d), sc_f, sh_f, g_f, conv_w[0],
                     conv_b[0].reshape(1, -1),
                     norm_final_g.reshape(1, -1), w_up[0].astype(BF16), w_down[0].astype(BF16),
                     tm=512, chunk=256)
```

```python
import functools
import math

import jax
import jax.numpy as jnp
from jax import lax
from jax.experimental import pallas as pl
from jax.experimental.pallas import tpu as pltpu

F32 = jnp.float32
BF16 = jnp.bfloat16

D_MODEL = 1024
MLA_HEADS = 8
MLA_Q_RANK = 384
MLA_KV_RANK = 256
MLA_NOPE = 64
MLA_ROPE = 32
MLA_V = 64
MLA_SCALE = 1.0 / math.sqrt(MLA_NOPE + MLA_ROPE)
ROPE_THETA = 10000.0
FOX_HEADS = 8
FOX_DIM = 64
FOX_SCALE = 1.0 / math.sqrt(FOX_DIM)
LOG2E = math.log2(math.e)
D_FF = 2816
CONV_WIDTH = 3
EPS = 1e-6
N_ADA = 6

LANES = 128
HEAD_PAD = LANES
N_HEADS = MLA_HEADS + FOX_HEADS
ROPE_HALF = MLA_ROPE // 2
ROPE_LO = MLA_NOPE
ROPE_HI = MLA_NOPE + ROPE_HALF
FLOGIT_LO = MLA_NOPE + MLA_ROPE
AUG_LO = FOX_DIM
ONES_ROWS = 16

VMEM_LIMIT = 56 * 1024 * 1024


def _silu(x):
    return x * (1.0 / (1.0 + jnp.exp(-x)))


def _sigmoid(x):
    return 1.0 / (1.0 + jnp.exp(-x))


def _rms_scale(x):
    return x * lax.rsqrt(jnp.mean(x * x, axis=-1, keepdims=True) + EPS)


def _ada_kernel(ct_ref, w_ref, b_ref, o_ref, *, batch):
    act = _silu(ct_ref[...])
    w = w_ref[...]
    for b in range(batch):
        o_ref[b:b + 1, :] = jnp.sum(w * act[:, b:b + 1], axis=0, keepdims=True) + b_ref[...]


def _ada_call(c, w_ada, b_ada):
    batch, d = c.shape
    n = w_ada.shape[1]
    tn = 1536
    return pl.pallas_call(
        functools.partial(_ada_kernel, batch=batch),
        grid=(n // tn,),
        in_specs=[pl.BlockSpec((d, batch), lambda j: (0, 0)),
                  pl.BlockSpec((d, tn), lambda j: (0, j)),
                  pl.BlockSpec((1, tn), lambda j: (0, j))],
        out_specs=pl.BlockSpec((batch, tn), lambda j: (0, j)),
        out_shape=jax.ShapeDtypeStruct((batch, n), F32),
        compiler_params=pltpu.CompilerParams(dimension_semantics=("arbitrary",),
                                             vmem_limit_bytes=VMEM_LIMIT),
        name="ada",
    )(c.T, w_ada, b_ada.reshape(1, n))


def _rope_tables(pos_col, freq_row):
    ang = pos_col * freq_row
    lane = lax.broadcasted_iota(jnp.int32, ang.shape, 1)
    cos = jnp.cos(ang)
    sin = jnp.sin(ang)
    s_lo = jnp.where((lane >= ROPE_LO) & (lane < ROPE_HI), -sin, 0.0)
    s_hi = jnp.where((lane >= ROPE_HI) & (lane < FLOGIT_LO), sin, 0.0)
    return cos, s_lo, s_hi


def _rope_block(t, cos, s_lo, s_hi):
    return (t * cos + pltpu.roll(t, LANES - ROPE_HALF, 1) * s_lo
            + pltpu.roll(t, ROPE_HALF, 1) * s_hi)


def _split3(x):
    hi = x.astype(BF16)
    r1 = x - hi.astype(F32)
    mid = r1.astype(BF16)
    lo = (r1 - mid.astype(F32)).astype(BF16)
    return hi, mid, lo


def _modulated_norm(x, gain_ref, sc_ref, sh_ref):
    return (_rms_scale(x) * (gain_ref[...] * (1.0 + sc_ref[0])) + sh_ref[0]).astype(BF16)


def _inproj_kernel(x_ref, pos_ref, ng_ref, sc_ref, sh_ref, freq_ref, bf_ref, qg_ref, kvg_ref,
                   wlat_ref, wmisc_ref, wfq_ref, wfk_ref, wfv_ref, wgate_ref,
                   wuq_ref, wukvk_ref, wukvv_ref, sq_ref, sk_ref, qc_ref, kc_ref,
                   qt_ref, k_ref, vt_ref, g_ref, carry_ref, *, tm):
    s_idx = pl.program_id(1)

    @pl.when(s_idx == 0)
    def _():
        carry_ref[...] = jnp.zeros_like(carry_ref)

    h = _modulated_norm(x_ref[0], ng_ref, sc_ref, sh_ref)

    def proj(w_ref, lo=None, hi=None):
        w = w_ref[...] if lo is None else w_ref[:, lo:hi]
        return jnp.dot(h, w, preferred_element_type=F32)

    lane = lax.broadcasted_iota(jnp.int32, (tm, LANES), 1)
    cos, s_lo, s_hi = _rope_tables(pos_ref[0], freq_ref[...])

    cqn = (_rms_scale(proj(wlat_ref, 0, MLA_Q_RANK)) * qg_ref[...]).astype(BF16)
    qm = jnp.dot(cqn, wuq_ref[...], preferred_element_type=F32)
    for hd in range(MLA_HEADS):
        blk = qm[:, hd * HEAD_PAD:(hd + 1) * HEAD_PAD]
        qt_ref[0, hd * HEAD_PAD:(hd + 1) * HEAD_PAD, :] = (
            _rope_block(blk, cos, s_lo, s_hi) * (MLA_SCALE * LOG2E)).T.astype(BF16)

    misc = proj(wmisc_ref)
    krope = _rope_block(jnp.where(lane < FLOGIT_LO, misc, 0.0), cos, s_lo, s_hi)
    ckvn = (_rms_scale(proj(wlat_ref, MLA_Q_RANK, MLA_Q_RANK + MLA_KV_RANK))
            * kvg_ref[...]).astype(BF16)
    kn = jnp.dot(ckvn, wukvk_ref[...], preferred_element_type=F32)
    for hd in range(MLA_HEADS):
        k_ref[0, :, hd * HEAD_PAD:(hd + 1) * HEAD_PAD] = (
            kn[:, hd * HEAD_PAD:(hd + 1) * HEAD_PAD] + krope).astype(BF16)
    vt_ref[0, 0:MLA_HEADS * MLA_V, :] = jnp.dot(
        ckvn, wukvv_ref[...], preferred_element_type=F32).T.astype(BF16)

    z = misc + bf_ref[...]
    logf = jnp.minimum(z, 0.0) - jnp.log1p(jnp.exp(-jnp.abs(z)))
    logf = jnp.where((lane >= FLOGIT_LO) & (lane < FLOGIT_LO + FOX_HEADS), logf, 0.0)
    row = lax.broadcasted_iota(jnp.int32, (tm, tm), 0)
    col = lax.broadcasted_iota(jnp.int32, (tm, tm), 1)
    tri = jnp.where(col <= row, 1.0, 0.0).astype(BF16)
    fcum = carry_ref[...] + sum(jnp.dot(tri, p, preferred_element_type=F32)
                                for p in _split3(logf))
    carry_ref[...] = fcum[tm - 1:tm, :]

    hi, mid, lo = (p.astype(F32) for p in _split3(fcum * LOG2E))
    bias3 = (hi + pltpu.roll(mid, FOX_HEADS, 1) + pltpu.roll(lo, 2 * FOX_HEADS, 1)).astype(BF16)
    qt_ref[0, MLA_HEADS * HEAD_PAD:, :] = (
        proj(wfq_ref) * (FOX_SCALE * LOG2E)
        + jnp.dot(bias3, sq_ref[...], preferred_element_type=F32) + qc_ref[...]).T.astype(BF16)
    k_ref[0, :, MLA_HEADS * HEAD_PAD:] = (
        proj(wfk_ref)
        + jnp.dot(bias3, sk_ref[...], preferred_element_type=F32) + kc_ref[...]).astype(BF16)
    vt_ref[0, MLA_HEADS * MLA_V:, :] = proj(wfv_ref).T.astype(BF16)

    g_ref[0] = _sigmoid(proj(wgate_ref)).astype(BF16)


def _resident(shape):
    return pl.BlockSpec(shape, lambda bi, si: (0,) * len(shape), pipeline_mode=pl.Buffered(1))


def _inproj_call(x, pos_f, norm_g, sc_mix, sh_mix, freq_row, bf_row, qg, kvg, weights, tm):
    b, s, d = x.shape
    tok = lambda w: pl.BlockSpec((1, tm, w), lambda bi, si: (bi, si, 0))
    vec = pl.BlockSpec((1, 1, d), lambda bi, si: (bi, 0, 0))
    return pl.pallas_call(
        functools.partial(_inproj_kernel, tm=tm),
        grid=(b, s // tm),
        in_specs=[tok(d), tok(1), _resident((1, d)), vec, vec, _resident((1, LANES)),
                  _resident((1, LANES)), _resident((1, MLA_Q_RANK)), _resident((1, MLA_KV_RANK))]
                 + [_resident(w.shape) for w in weights],
        out_specs=[pl.BlockSpec((1, N_HEADS * HEAD_PAD, tm), lambda bi, si: (bi, 0, si)),
                   tok(N_HEADS * HEAD_PAD),
                   pl.BlockSpec((1, N_HEADS * FOX_DIM, tm), lambda bi, si: (bi, 0, si)),
                   tok(2 * d)],
        out_shape=[jax.ShapeDtypeStruct((b, N_HEADS * HEAD_PAD, s), BF16),
                   jax.ShapeDtypeStruct((b, s, N_HEADS * HEAD_PAD), BF16),
                   jax.ShapeDtypeStruct((b, N_HEADS * FOX_DIM, s), BF16),
                   jax.ShapeDtypeStruct((b, s, 2 * d), BF16)],
        scratch_shapes=[pltpu.VMEM((1, LANES), F32)],
        compiler_params=pltpu.CompilerParams(dimension_semantics=("arbitrary", "arbitrary"),
                                             vmem_limit_bytes=VMEM_LIMIT),
        name="inproj",
    )(x, pos_f, norm_g, sc_mix, sh_mix, freq_row, bf_row, qg, kvg, *weights)


def _attn_kernel(qt_ref, k_ref, vt_ref, o_ref, st_ref, mx_ref, m_ref, acc_ref, *, tq, tk, nsub):
    qi = pl.program_id(2)
    assert tq == tk
    row = lax.broadcasted_iota(jnp.int32, (tk, tq), 0)
    col = lax.broadcasted_iota(jnp.int32, (tk, tq), 1)
    causal = row <= col
    ones_rows = jnp.ones((ONES_ROWS, tk), BF16)
    streams = [(hh, sub) for sub in range(nsub) for hh in range(2)]

    def scores(idx, j, slot, full):
        hh, sub = streams[idx]
        start = pl.multiple_of(j * tk, tk)
        qt = qt_ref[0, hh * HEAD_PAD:(hh + 1) * HEAD_PAD, sub * tq:(sub + 1) * tq]
        k = k_ref[0, pl.ds(start, tk), hh * HEAD_PAD:(hh + 1) * HEAD_PAD]
        st = jnp.dot(k, qt, preferred_element_type=F32)
        if full is False:
            st = jnp.where(causal, st, -jnp.inf)
        elif full is not True:
            st = jnp.where(row <= col + jnp.where(full, tk, 0), st, -jnp.inf)
        st_ref[slot, idx] = st
        mx_ref[slot, idx] = jnp.max(st, axis=0, keepdims=True)

    def update(idx, j, slot):
        hh, _ = streams[idx]
        start = pl.multiple_of(j * tk, tk)
        m = m_ref[idx]
        m_new = jnp.maximum(m, mx_ref[slot, idx])
        m_ref[idx] = m_new
        p = jnp.exp2(st_ref[slot, idx] - m_new).astype(BF16)
        vt = vt_ref[0, hh * MLA_V:(hh + 1) * MLA_V, pl.ds(start, tk)]
        vt_aug = jnp.concatenate([vt, ones_rows], axis=0)
        acc_ref[idx] = (jnp.exp2(m - m_new) * acc_ref[idx]
                        + jnp.dot(vt_aug, p, preferred_element_type=F32))

    def step(j, slot, diag_sub=None, next_is_full=None):
        first = 0 if diag_sub is None else diag_sub
        for idx, (_, sub) in enumerate(streams):
            if diag_sub is None:
                scores(idx, j + 1, 1 - slot, full=next_is_full if sub == 0 else True)
            elif sub > first:
                scores(idx, j + 1, 1 - slot, full=(sub != diag_sub + 1))
            if sub >= first:
                update(idx, j, slot)

    m_ref[...] = jnp.full(m_ref.shape, -jnp.inf, F32)
    acc_ref[...] = jnp.zeros(acc_ref.shape, F32)
    assert nsub % 2 == 0
    n_full = qi * nsub
    for idx, (_, sub) in enumerate(streams):
        scores(idx, 0, 0, full=(qi > 0) if sub == 0 else True)

    def body(i, carry):
        for u in range(nsub):
            step(nsub * i + u, u % 2, next_is_full=True if u < nsub - 1 else i < qi - 1)
        return carry

    lax.fori_loop(0, qi, body, 0)
    for r in range(nsub):
        step(n_full + r, r % 2, diag_sub=r)

    for sub in range(nsub):
        outs = []
        for hh in range(2):
            acc = acc_ref[streams.index((hh, sub))]
            outs.append(acc[0:MLA_V, :] / acc[MLA_V:MLA_V + 1, :])
        o_ref[0, sub * tq:(sub + 1) * tq, :] = jnp.concatenate(outs, axis=0).T.astype(BF16)


def _attn_call(qt_all, k_all, vt_all, tq, nsub):
    b, s, _ = k_all.shape
    pairs = N_HEADS // 2
    bq = tq * nsub
    return pl.pallas_call(
        functools.partial(_attn_kernel, tq=tq, tk=tq, nsub=nsub),
        grid=(b, pairs, s // bq),
        in_specs=[pl.BlockSpec((1, 2 * HEAD_PAD, bq), lambda bi, pi, qi: (bi, pi, qi)),
                  pl.BlockSpec((1, s, 2 * HEAD_PAD), lambda bi, pi, qi: (bi, 0, pi)),
                  pl.BlockSpec((1, LANES, s), lambda bi, pi, qi: (bi, pi, 0))],
        out_specs=pl.BlockSpec((1, bq, LANES), lambda bi, pi, qi: (bi, qi, pi)),
        out_shape=jax.ShapeDtypeStruct((b, s, pairs * LANES), BF16),
        scratch_shapes=[pltpu.VMEM((2, 2 * nsub, tq, tq), F32),
                        pltpu.VMEM((2, 2 * nsub, 1, tq), F32),
                        pltpu.VMEM((2 * nsub, 1, tq), F32),
                        pltpu.VMEM((2 * nsub, MLA_V + ONES_ROWS, tq), F32)],
        compiler_params=pltpu.CompilerParams(
            dimension_semantics=("arbitrary", "arbitrary", "arbitrary"),
            vmem_limit_bytes=VMEM_LIMIT),
        name="attn",
    )(qt_all, k_all, vt_all)


def _outproj_kernel(o_ref, g_ref, x_ref, gm_ref, wm_ref, wf_ref, wo_ref, x1_ref):
    half = MLA_HEADS * MLA_V
    o = o_ref[0]
    g = g_ref[0].astype(F32)
    y = (g[:, :D_MODEL] * jnp.dot(o[:, :half], wm_ref[...], preferred_element_type=F32)
         + g[:, D_MODEL:] * jnp.dot(o[:, half:], wf_ref[...], preferred_element_type=F32))
    mix = jnp.dot(y.astype(BF16), wo_ref[...], preferred_element_type=F32)
    x1_ref[0] = x_ref[0] + gm_ref[0] * mix


def _outproj_call(o_all, g_all, x, g_m, wm, wf, wo, tm):
    b, s, d = x.shape
    const = _resident
    tok = lambda w: pl.BlockSpec((1, tm, w), lambda bi, si: (bi, si, 0))
    vec = pl.BlockSpec((1, 1, d), lambda bi, si: (bi, 0, 0))
    return pl.pallas_call(
        _outproj_kernel,
        grid=(b, s // tm),
        in_specs=[tok(o_all.shape[-1]), tok(g_all.shape[-1]), tok(d), vec,
                  const(wm.shape), const(wf.shape), const(wo.shape)],
        out_specs=tok(d),
        out_shape=jax.ShapeDtypeStruct((b, s, d), F32),
        compiler_params=pltpu.CompilerParams(dimension_semantics=("arbitrary", "arbitrary"),
                                             vmem_limit_bytes=VMEM_LIMIT),
        name="outproj",
    )(o_all, g_all, x, g_m, wm, wf, wo)


def _shift_rows(u, prev, k):
    tm = u.shape[0]
    body = pltpu.roll(u, k, 0)
    head = jnp.where(lax.broadcasted_iota(jnp.int32, prev.shape, 0) < k,
                     pltpu.roll(prev, k, 0), body[0:8, :])
    return jnp.concatenate([head, body[8:tm, :]], axis=0)


def _ffn_kernel(x1_ref, ng_ref, sc_ref, sh_ref, gf_ref, cw_ref, cb_ref, fg_ref, wup_ref, wdn_ref,
                out_ref, tail_ref, act_ref, *, chunk):
    s_idx = pl.program_id(1)

    @pl.when(s_idx == 0)
    def _():
        tail_ref[...] = jnp.zeros_like(tail_ref)

    x1 = x1_ref[0]
    tm = x1.shape[0]
    h = _modulated_norm(x1, ng_ref, sc_ref, sh_ref)

    def conv(c0):
        u = jnp.dot(h, wup_ref[:, c0:c0 + chunk], preferred_element_type=F32)
        prev = tail_ref[:, c0:c0 + chunk]
        tail_ref[:, c0:c0 + chunk] = u[tm - 8:tm, :]
        w = cw_ref[:, c0:c0 + chunk]
        return (cb_ref[:, c0:c0 + chunk] + w[0:1] * _shift_rows(u, prev, 2)
                + w[1:2] * _shift_rows(u, prev, 1) + w[2:3] * u)

    for c0 in range(0, D_FF, chunk):
        act_ref[:, c0:c0 + chunk] = (_silu(conv(c0)) * conv(D_FF + c0)).astype(BF16)

    y = jnp.dot(act_ref[...], wdn_ref[...], preferred_element_type=F32)
    x2 = x1 + gf_ref[0] * y
    out_ref[0] = _rms_scale(x2) * fg_ref[...]


def _ffn_call(x1, norm_g, sc_ffn, sh_ffn, g_f, conv_w, conv_b, final_g, wup, wdn, tm, chunk):
    b, s, d = x1.shape
    const = _resident
    tok = pl.BlockSpec((1, tm, d), lambda bi, si: (bi, si, 0))
    vec = pl.BlockSpec((1, 1, d), lambda bi, si: (bi, 0, 0))
    return pl.pallas_call(
        functools.partial(_ffn_kernel, chunk=chunk),
        grid=(b, s // tm),
        in_specs=[tok, const((1, d)), vec, vec, vec, const(conv_w.shape), const(conv_b.shape),
                  const(final_g.shape), const(wup.shape), const(wdn.shape)],
        out_specs=tok,
        out_shape=jax.ShapeDtypeStruct((b, s, d), F32),
        scratch_shapes=[pltpu.VMEM((8, 2 * D_FF), F32), pltpu.VMEM((tm, D_FF), BF16)],
        compiler_params=pltpu.CompilerParams(dimension_semantics=("arbitrary", "arbitrary"),
                                             vmem_limit_bytes=VMEM_LIMIT),
        name="ffn",
    )(x1, norm_g, sc_ffn, sh_ffn, g_f, conv_w, conv_b, final_g, wup, wdn)


def _pad_heads(w, heads, width):
    k = w.shape[0]
    w = w.reshape(k, heads, width)
    return jnp.pad(w, ((0, 0), (0, 0), (0, HEAD_PAD - width))).reshape(k, heads * HEAD_PAD)


def _prep_w_in(w_in):
    d = w_in.shape[0]
    splits = (MLA_Q_RANK, MLA_KV_RANK, MLA_ROPE, FOX_HEADS * FOX_DIM, FOX_HEADS * FOX_DIM,
              FOX_HEADS * FOX_DIM, FOX_HEADS, D_MODEL, D_MODEL)
    offs = [0]
    for n in splits:
        offs.append(offs[-1] + n)
    piece = lambda i, j=None: w_in[:, offs[i]:offs[i + 1 if j is None else j]]
    misc = jnp.concatenate([jnp.zeros((d, MLA_NOPE), w_in.dtype), piece(2), piece(6),
                            jnp.zeros((d, LANES - FLOGIT_LO - FOX_HEADS), w_in.dtype)], axis=1)
    return [piece(0, 2).astype(BF16), misc.astype(BF16),
            _pad_heads(piece(3), FOX_HEADS, FOX_DIM).astype(BF16),
            _pad_heads(piece(4), FOX_HEADS, FOX_DIM).astype(BF16),
            piece(5).astype(BF16), piece(7, 9).astype(BF16)]


def _bias_placement():
    part, head = jnp.meshgrid(jnp.arange(3), jnp.arange(FOX_HEADS), indexing="ij")
    rows = (FLOGIT_LO + part * FOX_HEADS + head).reshape(-1)
    q_cols = (head * HEAD_PAD + AUG_LO + part).reshape(-1)
    k_cols = (head * HEAD_PAD + AUG_LO + 3 + part).reshape(-1)
    width = FOX_HEADS * HEAD_PAD
    sq = jnp.zeros((LANES, width), F32).at[rows, q_cols].set(1.0)
    sk = jnp.zeros((LANES, width), F32).at[rows, k_cols].set(-1.0)
    qc = jnp.zeros((1, width), F32).at[0, k_cols].set(1.0)
    kc = jnp.zeros((1, width), F32).at[0, q_cols].set(1.0)
    return sq.astype(BF16), sk.astype(BF16), qc, kc


def kernel(x, c, positions, w_ada, b_ada, norm_mix_g, w_in, q_norm_g, w_uq, kv_norm_g, w_ukv,
           b_forget, w_o_mla, w_o_fox, w_out, norm_ffn_g, w_up, conv_w, conv_b, w_down,
           norm_final_g):
    assert w_ada.shape[0] == 1, "single-layer block"
    b, s, d = x.shape
    ada = _ada_call(c, w_ada[0], b_ada[0])
    sh_m, sc_m, g_m, sh_f, sc_f, g_f = (ada[:, i * d:(i + 1) * d].reshape(b, 1, d)
                                        for i in range(N_ADA))

    inv_freq = ROPE_THETA ** (-jnp.arange(0, MLA_ROPE, 2, dtype=F32) / MLA_ROPE)
    freq_row = jnp.concatenate([jnp.zeros((MLA_NOPE,), F32), inv_freq, inv_freq,
                                jnp.zeros((LANES - FLOGIT_LO,), F32)]).reshape(1, LANES)
    bf_row = jnp.concatenate([jnp.zeros((FLOGIT_LO,), F32), b_forget[0].astype(F32),
                              jnp.zeros((LANES - FLOGIT_LO - FOX_HEADS,), F32)]).reshape(1, LANES)
    pos_f = positions.astype(F32).reshape(b, s, 1)

    wuq = _pad_heads(w_uq[0], MLA_HEADS, MLA_NOPE + MLA_ROPE).astype(BF16)
    wukv = w_ukv[0].reshape(MLA_KV_RANK, MLA_HEADS, MLA_NOPE + MLA_V)
    wukvk = _pad_heads(wukv[:, :, :MLA_NOPE].reshape(MLA_KV_RANK, -1), MLA_HEADS,
                       MLA_NOPE).astype(BF16)
    wukvv = wukv[:, :, MLA_NOPE:].reshape(MLA_KV_RANK, -1).astype(BF16)

    qt_all, k_all, vt_all, g_all = _inproj_call(
        x, pos_f, norm_mix_g[0].reshape(1, d), sc_m, sh_m, freq_row, bf_row,
        q_norm_g[0].reshape(1, -1), kv_norm_g[0].reshape(1, -1),
        _prep_w_in(w_in[0]) + [wuq, wukvk, wukvv] + list(_bias_placement()), tm=512)
    o_all = _attn_call(qt_all, k_all, vt_all, tq=256, nsub=4)
    x1 = _outproj_call(o_all, g_all, x, g_m, w_o_mla[0].astype(BF16), w_o_fox[0].astype(BF16),
                       w_out[0].astype(BF16), tm=512)
    return _ffn_call(x1, norm_ffn_g[0].reshape(1, d), sc_f, sh_f, g_f, conv_w[0],
                     conv_b[0].reshape(1, -1),
                     norm_final_g.reshape(1, -1), w_up[0].astype(BF16), w_down[0].astype(BF16),
                     tm=512, chunk=256)
```

```python
import functools
import math

import jax
import jax.numpy as jnp
import numpy as np
from jax import lax
from jax.experimental import pallas as pl
from jax.experimental.pallas import tpu as pltpu

F32 = jnp.float32
BF16 = jnp.bfloat16

D_MODEL = 1024
MLA_HEADS = 8
MLA_Q_RANK = 384
MLA_KV_RANK = 256
MLA_NOPE = 64
MLA_ROPE = 32
MLA_V = 64
MLA_SCALE = 1.0 / math.sqrt(MLA_NOPE + MLA_ROPE)
ROPE_THETA = 10000.0
FOX_HEADS = 8
FOX_DIM = 64
FOX_SCALE = 1.0 / math.sqrt(FOX_DIM)
LOG2E = math.log2(math.e)
D_FF = 2816
CONV_WIDTH = 3
EPS = 1e-6
N_ADA = 6

LANES = 128
HEAD_PAD = LANES
N_HEADS = MLA_HEADS + FOX_HEADS
ROPE_HALF = MLA_ROPE // 2
ROPE_LO = MLA_NOPE
ROPE_HI = MLA_NOPE + ROPE_HALF
FLOGIT_LO = MLA_NOPE + MLA_ROPE
AUG_LO = FOX_DIM
ONES_ROWS = 16

VMEM_LIMIT = 56 * 1024 * 1024


def _silu(x):
    return x * (1.0 / (1.0 + jnp.exp(-x)))


def _sigmoid(x):
    return 1.0 / (1.0 + jnp.exp(-x))


def _rms_scale(x):
    return x * lax.rsqrt(jnp.mean(x * x, axis=-1, keepdims=True) + EPS)


def _ada_kernel(ct_ref, w_ref, b_ref, o_ref, *, batch):
    act = _silu(ct_ref[...])
    w = w_ref[...]
    for b in range(batch):
        o_ref[b:b + 1, :] = jnp.sum(w * act[:, b:b + 1], axis=0, keepdims=True) + b_ref[...]


def _ada_call(c, w_ada, b_ada):
    batch, d = c.shape
    n = w_ada.shape[1]
    tn = 1536
    return pl.pallas_call(
        functools.partial(_ada_kernel, batch=batch),
        grid=(n // tn,),
        in_specs=[pl.BlockSpec((d, batch), lambda j: (0, 0)),
                  pl.BlockSpec((d, tn), lambda j: (0, j)),
                  pl.BlockSpec((1, tn), lambda j: (0, j))],
        out_specs=pl.BlockSpec((batch, tn), lambda j: (0, j)),
        out_shape=jax.ShapeDtypeStruct((batch, n), F32),
        compiler_params=pltpu.CompilerParams(dimension_semantics=("arbitrary",),
                                             vmem_limit_bytes=VMEM_LIMIT),
        name="ada",
    )(c.T, w_ada, b_ada.reshape(1, n))


def _rope_tables(pos_col, freq_row):
    ang = pos_col * freq_row
    lane = lax.broadcasted_iota(jnp.int32, ang.shape, 1)
    cos = jnp.cos(ang)
    sin = jnp.sin(ang)
    s_lo = jnp.where((lane >= ROPE_LO) & (lane < ROPE_HI), -sin, 0.0)
    s_hi = jnp.where((lane >= ROPE_HI) & (lane < FLOGIT_LO), sin, 0.0)
    return cos, s_lo, s_hi


def _rope_block(t, cos, s_lo, s_hi):
    return (t * cos + pltpu.roll(t, LANES - ROPE_HALF, 1) * s_lo
            + pltpu.roll(t, ROPE_HALF, 1) * s_hi)


def _split3(x):
    hi = x.astype(BF16)
    r1 = x - hi.astype(F32)
    mid = r1.astype(BF16)
    lo = (r1 - mid.astype(F32)).astype(BF16)
    return hi, mid, lo


def _modulated_norm(x, gain_ref, sc_ref, sh_ref):
    return (_rms_scale(x) * (gain_ref[...] * (1.0 + sc_ref[0])) + sh_ref[0]).astype(BF16)


def _inproj_kernel(x_ref, pos_ref, ng_ref, sc_ref, sh_ref, freq_ref, bf_ref, qg_ref, kvg_ref,
                   wlat_ref, wmisc_ref, wfq_ref, wfk_ref, wfv_ref, wgate_ref,
                   wuq_ref, wukvk_ref, wukvv_ref, sq_ref, sk_ref, qc_ref, kc_ref,
                   qt_ref, k_ref, vt_ref, g_ref, carry_ref, *, tm):
    s_idx = pl.program_id(1)

    @pl.when(s_idx == 0)
    def _():
        carry_ref[...] = jnp.zeros_like(carry_ref)

    h = _modulated_norm(x_ref[0], ng_ref, sc_ref, sh_ref)

    def proj(w_ref, lo=None, hi=None):
        w = w_ref[...] if lo is None else w_ref[:, lo:hi]
        return jnp.dot(h, w, preferred_element_type=F32)

    lane = lax.broadcasted_iota(jnp.int32, (tm, LANES), 1)
    pos_col = jnp.broadcast_to(pos_ref[0], (8, tm)).T[:, 0:1]
    cos, s_lo, s_hi = _rope_tables(pos_col, freq_ref[...])

    lat_q = proj(wlat_ref, 0, MLA_Q_RANK)
    lat_kv = proj(wlat_ref, MLA_Q_RANK, MLA_Q_RANK + MLA_KV_RANK)
    misc = proj(wmisc_ref)
    g_ref[0, :, 0:D_MODEL] = _sigmoid(proj(wgate_ref, 0, D_MODEL)).astype(BF16)

    cqn = (_rms_scale(lat_q) * qg_ref[...]).astype(BF16)
    ckvn = (_rms_scale(lat_kv) * kvg_ref[...]).astype(BF16)
    qm = jnp.dot(cqn, wuq_ref[...], preferred_element_type=F32)
    kn = jnp.dot(ckvn, wukvk_ref[...], preferred_element_type=F32)
    vm = jnp.dot(ckvn, wukvv_ref[...], preferred_element_type=F32)
    fv = proj(wfv_ref)
    for hd in range(MLA_HEADS):
        blk = qm[:, hd * HEAD_PAD:(hd + 1) * HEAD_PAD]
        qt_ref[0, hd * HEAD_PAD:(hd + 1) * HEAD_PAD, :] = (
            _rope_block(blk, cos, s_lo, s_hi) * (MLA_SCALE * LOG2E)).T.astype(BF16)

    krope = _rope_block(jnp.where(lane < FLOGIT_LO, misc, 0.0), cos, s_lo, s_hi)
    for hd in range(MLA_HEADS):
        k_ref[0, :, hd * HEAD_PAD:(hd + 1) * HEAD_PAD] = (
            kn[:, hd * HEAD_PAD:(hd + 1) * HEAD_PAD] + krope).astype(BF16)
    vt_ref[0, 0:MLA_HEADS * MLA_V, :] = vm.T.astype(BF16)
    vt_ref[0, MLA_HEADS * MLA_V:, :] = fv.T.astype(BF16)

    z = misc + bf_ref[...]
    logf = jnp.minimum(z, 0.0) - jnp.log1p(jnp.exp(-jnp.abs(z)))
    logf = jnp.where((lane >= FLOGIT_LO) & (lane < FLOGIT_LO + FOX_HEADS), logf, 0.0)
    row = lax.broadcasted_iota(jnp.int32, (tm, tm), 0)
    col = lax.broadcasted_iota(jnp.int32, (tm, tm), 1)
    tri = jnp.where(col <= row, 1.0, 0.0).astype(BF16)
    fcum = carry_ref[...] + sum(jnp.dot(tri, p, preferred_element_type=F32)
                                for p in _split3(logf))
    carry_ref[...] = fcum[tm - 1:tm, :]
    fq = proj(wfq_ref)
    fk = proj(wfk_ref)

    hi, mid, lo = (p.astype(F32) for p in _split3(fcum * LOG2E))
    bias3 = (hi + pltpu.roll(mid, FOX_HEADS, 1) + pltpu.roll(lo, 2 * FOX_HEADS, 1)).astype(BF16)
    qt_ref[0, MLA_HEADS * HEAD_PAD:, :] = (
        fq * (FOX_SCALE * LOG2E)
        + jnp.dot(bias3, sq_ref[...], preferred_element_type=F32) + qc_ref[...]).T.astype(BF16)
    k_ref[0, :, MLA_HEADS * HEAD_PAD:] = (
        fk + jnp.dot(bias3, sk_ref[...], preferred_element_type=F32) + kc_ref[...]).astype(BF16)
    g_ref[0, :, D_MODEL:] = _sigmoid(proj(wgate_ref, D_MODEL, 2 * D_MODEL)).astype(BF16)


def _resident(shape):
    return pl.BlockSpec(shape, lambda bi, si: (0,) * len(shape), pipeline_mode=pl.Buffered(1))


def _inproj_call(x, pos_f, norm_g, sc_mix, sh_mix, freq_row, bf_row, qg, kvg, weights, tm):
    b, s, d = x.shape
    tok = lambda w: pl.BlockSpec((1, tm, w), lambda bi, si: (bi, si, 0))
    vec = pl.BlockSpec((1, 1, d), lambda bi, si: (bi, 0, 0))
    return pl.pallas_call(
        functools.partial(_inproj_kernel, tm=tm),
        grid=(b, s // tm),
        in_specs=[tok(d), pl.BlockSpec((1, 1, tm), lambda bi, si: (bi, 0, si)),
                  _resident((1, d)), vec, vec, _resident((1, LANES)),
                  _resident((1, LANES)), _resident((1, MLA_Q_RANK)), _resident((1, MLA_KV_RANK))]
                 + [_resident(w.shape) for w in weights],
        out_specs=[pl.BlockSpec((1, N_HEADS * HEAD_PAD, tm), lambda bi, si: (bi, 0, si)),
                   tok(N_HEADS * HEAD_PAD),
                   pl.BlockSpec((1, N_HEADS * FOX_DIM, tm), lambda bi, si: (bi, 0, si)),
                   tok(2 * d)],
        out_shape=[jax.ShapeDtypeStruct((b, N_HEADS * HEAD_PAD, s), BF16),
                   jax.ShapeDtypeStruct((b, s, N_HEADS * HEAD_PAD), BF16),
                   jax.ShapeDtypeStruct((b, N_HEADS * FOX_DIM, s), BF16),
                   jax.ShapeDtypeStruct((b, s, 2 * d), BF16)],
        scratch_shapes=[pltpu.VMEM((1, LANES), F32)],
        compiler_params=pltpu.CompilerParams(dimension_semantics=("arbitrary", "arbitrary"),
                                             vmem_limit_bytes=VMEM_LIMIT),
        name="inproj",
    )(x, pos_f, norm_g, sc_mix, sh_mix, freq_row, bf_row, qg, kvg, *weights)


def _attn_kernel(qt_ref, k_ref, vt_ref, o_ref, st_ref, mx_ref, m_ref, acc_ref, *, tq, tk, nsub):
    qi = pl.program_id(2)
    assert tq == tk
    row = lax.broadcasted_iota(jnp.int32, (tk, tq), 0)
    col = lax.broadcasted_iota(jnp.int32, (tk, tq), 1)
    causal = row <= col
    ones_rows = jnp.ones((ONES_ROWS, tk), BF16)
    streams = [(hh, sub) for sub in range(nsub) for hh in range(2)]

    def scores(idx, j, slot, full):
        hh, sub = streams[idx]
        start = pl.multiple_of(j * tk, tk)
        qt = qt_ref[0, hh * HEAD_PAD:(hh + 1) * HEAD_PAD, sub * tq:(sub + 1) * tq]
        k = k_ref[0, pl.ds(start, tk), hh * HEAD_PAD:(hh + 1) * HEAD_PAD]
        st = jnp.dot(k, qt, preferred_element_type=F32)
        if full is False:
            st = jnp.where(causal, st, -jnp.inf)
        elif full is not True:
            st = jnp.where(row <= col + jnp.where(full, tk, 0), st, -jnp.inf)
        st_ref[slot, idx] = st
        mx_ref[slot, idx] = jnp.max(st, axis=0, keepdims=True)

    def update(idx, j, slot):
        hh, _ = streams[idx]
        start = pl.multiple_of(j * tk, tk)
        m = m_ref[idx]
        m_new = jnp.maximum(m, mx_ref[slot, idx])
        m_ref[idx] = m_new
        p = jnp.exp2(st_ref[slot, idx] - m_new).astype(BF16)
        vt = vt_ref[0, hh * MLA_V:(hh + 1) * MLA_V, pl.ds(start, tk)]
        vt_aug = jnp.concatenate([vt, ones_rows], axis=0)
        acc_ref[idx] = (jnp.exp2(m - m_new) * acc_ref[idx]
                        + jnp.dot(vt_aug, p, preferred_element_type=F32))

    def step(j, slot, diag_sub=None, next_is_full=None):
        first = 0 if diag_sub is None else diag_sub
        for idx, (_, sub) in enumerate(streams):
            if diag_sub is None:
                scores(idx, j + 1, 1 - slot, full=next_is_full if sub == 0 else True)
            elif sub > first:
                scores(idx, j + 1, 1 - slot, full=(sub != diag_sub + 1))
            if sub >= first:
                update(idx, j, slot)

    m_ref[...] = jnp.full(m_ref.shape, -jnp.inf, F32)
    acc_ref[...] = jnp.zeros(acc_ref.shape, F32)
    assert nsub % 2 == 0
    n_full = qi * nsub
    for idx, (_, sub) in enumerate(streams):
        scores(idx, 0, 0, full=(qi > 0) if sub == 0 else True)

    def body(i, carry):
        for u in range(nsub):
            step(nsub * i + u, u % 2, next_is_full=True if u < nsub - 1 else i < qi - 1)
        return carry

    lax.fori_loop(0, qi, body, 0)
    for r in range(nsub):
        step(n_full + r, r % 2, diag_sub=r)
        outs = []
        for hh in range(2):
            acc = acc_ref[streams.index((hh, r))]
            outs.append(acc[0:MLA_V, :] / acc[MLA_V:MLA_V + 1, :])
        o_ref[0, r * tq:(r + 1) * tq, :] = jnp.concatenate(outs, axis=0).T.astype(BF16)


def _attn_call(qt_all, k_all, vt_all, tq, nsub):
    b, s, _ = k_all.shape
    pairs = N_HEADS // 2
    bq = tq * nsub
    return pl.pallas_call(
        functools.partial(_attn_kernel, tq=tq, tk=tq, nsub=nsub),
        grid=(b, pairs, s // bq),
        in_specs=[pl.BlockSpec((1, 2 * HEAD_PAD, bq), lambda bi, pi, qi: (bi, pi, qi)),
                  pl.BlockSpec((1, s, 2 * HEAD_PAD), lambda bi, pi, qi: (bi, 0, pi)),
                  pl.BlockSpec((1, LANES, s), lambda bi, pi, qi: (bi, pi, 0))],
        out_specs=pl.BlockSpec((1, bq, LANES), lambda bi, pi, qi: (bi, qi, pi)),
        out_shape=jax.ShapeDtypeStruct((b, s, pairs * LANES), BF16),
        scratch_shapes=[pltpu.VMEM((2, 2 * nsub, tq, tq), F32),
                        pltpu.VMEM((2, 2 * nsub, 1, tq), F32),
                        pltpu.VMEM((2 * nsub, 1, tq), F32),
                        pltpu.VMEM((2 * nsub, MLA_V + ONES_ROWS, tq), F32)],
        compiler_params=pltpu.CompilerParams(
            dimension_semantics=("arbitrary", "arbitrary", "arbitrary"),
            vmem_limit_bytes=VMEM_LIMIT),
        name="attn",
    )(qt_all, k_all, vt_all)


def _outproj_kernel(o_ref, g_ref, x_ref, gm_ref, wm_ref, wf_ref, wo_ref, x1_ref):
    half = MLA_HEADS * MLA_V
    o = o_ref[0]
    g = g_ref[0].astype(F32)
    y = (g[:, :D_MODEL] * jnp.dot(o[:, :half], wm_ref[...], preferred_element_type=F32)
         + g[:, D_MODEL:] * jnp.dot(o[:, half:], wf_ref[...], preferred_element_type=F32))
    mix = jnp.dot(y.astype(BF16), wo_ref[...], preferred_element_type=F32)
    x1_ref[0] = x_ref[0] + gm_ref[0] * mix


def _outproj_call(o_all, g_all, x, g_m, wm, wf, wo, tm):
    b, s, d = x.shape
    const = _resident
    tok = lambda w: pl.BlockSpec((1, tm, w), lambda bi, si: (bi, si, 0))
    vec = pl.BlockSpec((1, 1, d), lambda bi, si: (bi, 0, 0))
    return pl.pallas_call(
        _outproj_kernel,
        grid=(b, s // tm),
        in_specs=[tok(o_all.shape[-1]), tok(g_all.shape[-1]), tok(d), vec,
                  const(wm.shape), const(wf.shape), const(wo.shape)],
        out_specs=tok(d),
        out_shape=jax.ShapeDtypeStruct((b, s, d), F32),
        compiler_params=pltpu.CompilerParams(dimension_semantics=("arbitrary", "arbitrary"),
                                             vmem_limit_bytes=VMEM_LIMIT),
        name="outproj",
    )(o_all, g_all, x, g_m, wm, wf, wo)


def _shift_rows(u, prev, k):
    tm = u.shape[0]
    body = pltpu.roll(u, k, 0)
    head = jnp.where(lax.broadcasted_iota(jnp.int32, prev.shape, 0) < k,
                     pltpu.roll(prev, k, 0), body[0:8, :])
    return jnp.concatenate([head, body[8:tm, :]], axis=0)


def _ffn_kernel(x1_ref, ng_ref, sc_ref, sh_ref, gf_ref, cw_ref, cb_ref, fg_ref, wup_ref, wdn_ref,
                out_ref, tail_ref, act_ref, *, chunk):
    s_idx = pl.program_id(1)

    @pl.when(s_idx == 0)
    def _():
        tail_ref[...] = jnp.zeros_like(tail_ref)

    x1 = x1_ref[0]
    tm = x1.shape[0]
    h = _modulated_norm(x1, ng_ref, sc_ref, sh_ref)

    def conv(c0):
        u = jnp.dot(h, wup_ref[:, c0:c0 + chunk], preferred_element_type=F32)
        prev = tail_ref[:, c0:c0 + chunk]
        tail_ref[:, c0:c0 + chunk] = u[tm - 8:tm, :]
        w = cw_ref[:, c0:c0 + chunk]
        return (cb_ref[:, c0:c0 + chunk] + w[0:1] * _shift_rows(u, prev, 2)
                + w[1:2] * _shift_rows(u, prev, 1) + w[2:3] * u)

    for c0 in range(0, D_FF, chunk):
        act_ref[:, c0:c0 + chunk] = (_silu(conv(c0)) * conv(D_FF + c0)).astype(BF16)

    y = jnp.dot(act_ref[...], wdn_ref[...], preferred_element_type=F32)
    x2 = x1 + gf_ref[0] * y
    out_ref[0] = _rms_scale(x2) * fg_ref[...]


def _ffn_call(x1, norm_g, sc_ffn, sh_ffn, g_f, conv_w, conv_b, final_g, wup, wdn, tm, chunk):
    b, s, d = x1.shape
    const = _resident
    tok = pl.BlockSpec((1, tm, d), lambda bi, si: (bi, si, 0))
    vec = pl.BlockSpec((1, 1, d), lambda bi, si: (bi, 0, 0))
    return pl.pallas_call(
        functools.partial(_ffn_kernel, chunk=chunk),
        grid=(b, s // tm),
        in_specs=[tok, const((1, d)), vec, vec, vec, const(conv_w.shape), const(conv_b.shape),
                  const(final_g.shape), const(wup.shape), const(wdn.shape)],
        out_specs=tok,
        out_shape=jax.ShapeDtypeStruct((b, s, d), F32),
        scratch_shapes=[pltpu.VMEM((8, 2 * D_FF), F32), pltpu.VMEM((tm, D_FF), BF16)],
        compiler_params=pltpu.CompilerParams(dimension_semantics=("arbitrary", "arbitrary"),
                                             vmem_limit_bytes=VMEM_LIMIT),
        name="ffn",
    )(x1, norm_g, sc_ffn, sh_ffn, g_f, conv_w, conv_b, final_g, wup, wdn)


def _pad_heads(w, heads, width):
    k = w.shape[0]
    w = w.reshape(k, heads, width)
    return jnp.pad(w, ((0, 0), (0, 0), (0, HEAD_PAD - width))).reshape(k, heads * HEAD_PAD)


def _prep_w_in(w_in):
    d = w_in.shape[0]
    splits = (MLA_Q_RANK, MLA_KV_RANK, MLA_ROPE, FOX_HEADS * FOX_DIM, FOX_HEADS * FOX_DIM,
              FOX_HEADS * FOX_DIM, FOX_HEADS, D_MODEL, D_MODEL)
    offs = [0]
    for n in splits:
        offs.append(offs[-1] + n)
    w_in = w_in.astype(BF16)
    piece = lambda i, j=None: w_in[:, offs[i]:offs[i + 1 if j is None else j]]
    misc = jnp.concatenate([jnp.zeros((d, MLA_NOPE), BF16), piece(2), piece(6),
                            jnp.zeros((d, LANES - FLOGIT_LO - FOX_HEADS), BF16)], axis=1)
    return [piece(0, 2), misc, _pad_heads(piece(3), FOX_HEADS, FOX_DIM),
            _pad_heads(piece(4), FOX_HEADS, FOX_DIM), piece(5), piece(7, 9)]


def _bias_placement():
    part, head = np.meshgrid(np.arange(3), np.arange(FOX_HEADS), indexing="ij")
    rows = (FLOGIT_LO + part * FOX_HEADS + head).reshape(-1)
    q_cols = (head * HEAD_PAD + AUG_LO + part).reshape(-1)
    k_cols = (head * HEAD_PAD + AUG_LO + 3 + part).reshape(-1)
    width = FOX_HEADS * HEAD_PAD
    sq, sk = np.zeros((LANES, width), np.float32), np.zeros((LANES, width), np.float32)
    qc, kc = np.zeros((1, width), np.float32), np.zeros((1, width), np.float32)
    sq[rows, q_cols] = 1.0
    sk[rows, k_cols] = -1.0
    qc[0, k_cols] = 1.0
    kc[0, q_cols] = 1.0
    return jnp.asarray(sq, BF16), jnp.asarray(sk, BF16), jnp.asarray(qc), jnp.asarray(kc)


def kernel(x, c, positions, w_ada, b_ada, norm_mix_g, w_in, q_norm_g, w_uq, kv_norm_g, w_ukv,
           b_forget, w_o_mla, w_o_fox, w_out, norm_ffn_g, w_up, conv_w, conv_b, w_down,
           norm_final_g):
    assert w_ada.shape[0] == 1, "single-layer block"
    b, s, d = x.shape
    ada = _ada_call(c, w_ada[0], b_ada[0])
    sh_m, sc_m, g_m, sh_f, sc_f, g_f = (ada[:, i * d:(i + 1) * d].reshape(b, 1, d)
                                        for i in range(N_ADA))

    inv_freq = ROPE_THETA ** (-jnp.arange(0, MLA_ROPE, 2, dtype=F32) / MLA_ROPE)
    freq_row = jnp.concatenate([jnp.zeros((MLA_NOPE,), F32), inv_freq, inv_freq,
                                jnp.zeros((LANES - FLOGIT_LO,), F32)]).reshape(1, LANES)
    bf_row = jnp.concatenate([jnp.zeros((FLOGIT_LO,), F32), b_forget[0].astype(F32),
                              jnp.zeros((LANES - FLOGIT_LO - FOX_HEADS,), F32)]).reshape(1, LANES)
    pos_f = positions.astype(F32).reshape(b, 1, s)

    wuq = _pad_heads(w_uq[0], MLA_HEADS, MLA_NOPE + MLA_ROPE).astype(BF16)
    wukv = w_ukv[0].reshape(MLA_KV_RANK, MLA_HEADS, MLA_NOPE + MLA_V)
    wukvk = _pad_heads(wukv[:, :, :MLA_NOPE].reshape(MLA_KV_RANK, -1), MLA_HEADS,
                       MLA_NOPE).astype(BF16)
    wukvv = wukv[:, :, MLA_NOPE:].reshape(MLA_KV_RANK, -1).astype(BF16)

    qt_all, k_all, vt_all, g_all = _inproj_call(
        x, pos_f, norm_mix_g[0].reshape(1, d), sc_m, sh_m, freq_row, bf_row,
        q_norm_g[0].reshape(1, -1), kv_norm_g[0].reshape(1, -1),
        _prep_w_in(w_in[0]) + [wuq, wukvk, wukvv] + list(_bias_placement()), tm=512)
    o_all = _attn_call(qt_all, k_all, vt_all, tq=256, nsub=4)
    x1 = _outproj_call(o_all, g_all, x, g_m, w_o_mla[0].astype(BF16), w_o_fox[0].astype(BF16),
                       w_out[0].astype(BF16), tm=512)
    return _ffn_call(x1, norm_ffn_g[0].reshape(1, d), sc_f, sh_f, g_f, conv_w[0],
                     conv_b[0].reshape(1, -1),
                     norm_final_g.reshape(1, -1), w_up[0].astype(BF16), w_down[0].astype(BF16),
                     tm=512, chunk=256)
```

```python
import functools
import math

import jax
import jax.numpy as jnp
import numpy as np
from jax import lax
from jax.experimental import pallas as pl
from jax.experimental.pallas import tpu as pltpu

F32 = jnp.float32
BF16 = jnp.bfloat16

D_MODEL = 1024
MLA_HEADS = 8
MLA_Q_RANK = 384
MLA_KV_RANK = 256
MLA_NOPE = 64
MLA_ROPE = 32
MLA_V = 64
MLA_SCALE = 1.0 / math.sqrt(MLA_NOPE + MLA_ROPE)
ROPE_THETA = 10000.0
FOX_HEADS = 8
FOX_DIM = 64
FOX_SCALE = 1.0 / math.sqrt(FOX_DIM)
LOG2E = math.log2(math.e)
D_FF = 2816
CONV_WIDTH = 3
EPS = 1e-6
N_ADA = 6

LANES = 128
HEAD_PAD = LANES
N_HEADS = MLA_HEADS + FOX_HEADS
ROPE_HALF = MLA_ROPE // 2
ROPE_LO = MLA_NOPE
ROPE_HI = MLA_NOPE + ROPE_HALF
FLOGIT_LO = MLA_NOPE + MLA_ROPE
AUG_LO = FOX_DIM
ONES_ROWS = 16

VMEM_LIMIT = 56 * 1024 * 1024


def _silu(x):
    return x * (1.0 / (1.0 + jnp.exp(-x)))


def _sigmoid(x):
    return 1.0 / (1.0 + jnp.exp(-x))


def _rms_scale(x):
    return x * lax.rsqrt(jnp.mean(x * x, axis=-1, keepdims=True) + EPS)


def _ada_kernel(ct_ref, w_ref, b_ref, o_ref, *, batch):
    act = _silu(ct_ref[...])
    w = w_ref[...]
    for b in range(batch):
        o_ref[b:b + 1, :] = jnp.sum(w * act[:, b:b + 1], axis=0, keepdims=True) + b_ref[...]


def _ada_call(c, w_ada, b_ada):
    batch, d = c.shape
    n = w_ada.shape[1]
    tn = 1536
    return pl.pallas_call(
        functools.partial(_ada_kernel, batch=batch),
        grid=(n // tn,),
        in_specs=[pl.BlockSpec((d, batch), lambda j: (0, 0)),
                  pl.BlockSpec((d, tn), lambda j: (0, j)),
                  pl.BlockSpec((1, tn), lambda j: (0, j))],
        out_specs=pl.BlockSpec((batch, tn), lambda j: (0, j)),
        out_shape=jax.ShapeDtypeStruct((batch, n), F32),
        compiler_params=pltpu.CompilerParams(dimension_semantics=("arbitrary",),
                                             vmem_limit_bytes=VMEM_LIMIT),
        name="ada",
    )(c.T, w_ada, b_ada.reshape(1, n))


def _rope_tables(pos_col, freq_row):
    ang = pos_col * freq_row
    lane = lax.broadcasted_iota(jnp.int32, ang.shape, 1)
    cos = jnp.cos(ang)
    sin = jnp.sin(ang)
    s_lo = jnp.where((lane >= ROPE_LO) & (lane < ROPE_HI), -sin, 0.0)
    s_hi = jnp.where((lane >= ROPE_HI) & (lane < FLOGIT_LO), sin, 0.0)
    return cos, s_lo, s_hi


def _rope_block(t, cos, s_lo, s_hi):
    return (t * cos + pltpu.roll(t, LANES - ROPE_HALF, 1) * s_lo
            + pltpu.roll(t, ROPE_HALF, 1) * s_hi)


def _split3(x):
    hi = x.astype(BF16)
    r1 = x - hi.astype(F32)
    mid = r1.astype(BF16)
    lo = (r1 - mid.astype(F32)).astype(BF16)
    return hi, mid, lo


def _modulated_norm(x, gain_ref, sc_ref, sh_ref):
    return (_rms_scale(x) * (gain_ref[...] * (1.0 + sc_ref[0])) + sh_ref[0]).astype(BF16)


def _inproj_kernel(x_ref, pos_ref, ng_ref, sc_ref, sh_ref, freq_ref, bf_ref, qg_ref, kvg_ref,
                   wlat_ref, wmisc_ref, wfq_ref, wfk_ref, wfv_ref, wgate_ref,
                   wuq_ref, wukvk_ref, wukvv_ref, sq_ref, sk_ref, qc_ref, kc_ref,
                   qt_ref, k_ref, vt_ref, g_ref, carry_ref, *, tm):
    s_idx = pl.program_id(1)

    @pl.when(s_idx == 0)
    def _():
        carry_ref[...] = jnp.zeros_like(carry_ref)

    h = _modulated_norm(x_ref[0], ng_ref, sc_ref, sh_ref)

    def proj(w_ref, lo=None, hi=None):
        w = w_ref[...] if lo is None else w_ref[:, lo:hi]
        return jnp.dot(h, w, preferred_element_type=F32)

    lane = lax.broadcasted_iota(jnp.int32, (tm, LANES), 1)
    pos_col = jnp.broadcast_to(pos_ref[0], (8, tm)).T[:, 0:1]
    cos, s_lo, s_hi = _rope_tables(pos_col, freq_ref[...])

    lat_q = proj(wlat_ref, 0, MLA_Q_RANK)
    lat_kv = proj(wlat_ref, MLA_Q_RANK, MLA_Q_RANK + MLA_KV_RANK)
    misc = proj(wmisc_ref)
    g_ref[0, :, 0:D_MODEL] = _sigmoid(proj(wgate_ref, 0, D_MODEL)).astype(BF16)

    cqn = (_rms_scale(lat_q) * qg_ref[...]).astype(BF16)
    ckvn = (_rms_scale(lat_kv) * kvg_ref[...]).astype(BF16)
    qm = jnp.dot(cqn, wuq_ref[...], preferred_element_type=F32)
    kn = jnp.dot(ckvn, wukvk_ref[...], preferred_element_type=F32)
    vm = jnp.dot(ckvn, wukvv_ref[...], preferred_element_type=F32)
    fv = proj(wfv_ref)
    for hd in range(MLA_HEADS):
        blk = qm[:, hd * HEAD_PAD:(hd + 1) * HEAD_PAD]
        qt_ref[0, hd * HEAD_PAD:(hd + 1) * HEAD_PAD, :] = (
            _rope_block(blk, cos, s_lo, s_hi) * (MLA_SCALE * LOG2E)).T.astype(BF16)

    krope = _rope_block(jnp.where(lane < FLOGIT_LO, misc, 0.0), cos, s_lo, s_hi)
    for hd in range(MLA_HEADS):
        k_ref[0, :, hd * HEAD_PAD:(hd + 1) * HEAD_PAD] = (
            kn[:, hd * HEAD_PAD:(hd + 1) * HEAD_PAD] + krope).astype(BF16)
    vt_ref[0, 0:MLA_HEADS * MLA_V, :] = vm.T.astype(BF16)
    vt_ref[0, MLA_HEADS * MLA_V:, :] = fv.T.astype(BF16)

    z = misc + bf_ref[...]
    logf = jnp.minimum(z, 0.0) - jnp.log1p(jnp.exp(-jnp.abs(z)))
    logf = jnp.where((lane >= FLOGIT_LO) & (lane < FLOGIT_LO + FOX_HEADS), logf, 0.0)
    row = lax.broadcasted_iota(jnp.int32, (tm, tm), 0)
    col = lax.broadcasted_iota(jnp.int32, (tm, tm), 1)
    tri = jnp.where(col <= row, 1.0, 0.0).astype(BF16)
    fcum = carry_ref[...] + sum(jnp.dot(tri, p, preferred_element_type=F32)
                                for p in _split3(logf))
    carry_ref[...] = fcum[tm - 1:tm, :]
    fq = proj(wfq_ref)
    fk = proj(wfk_ref)

    hi, mid, lo = (p.astype(F32) for p in _split3(fcum * LOG2E))
    bias3 = (hi + pltpu.roll(mid, FOX_HEADS, 1) + pltpu.roll(lo, 2 * FOX_HEADS, 1)).astype(BF16)
    qt_ref[0, MLA_HEADS * HEAD_PAD:, :] = (
        fq * (FOX_SCALE * LOG2E)
        + jnp.dot(bias3, sq_ref[...], preferred_element_type=F32) + qc_ref[...]).T.astype(BF16)
    k_ref[0, :, MLA_HEADS * HEAD_PAD:] = (
        fk + jnp.dot(bias3, sk_ref[...], preferred_element_type=F32) + kc_ref[...]).astype(BF16)
    g_ref[0, :, D_MODEL:] = _sigmoid(proj(wgate_ref, D_MODEL, 2 * D_MODEL)).astype(BF16)


def _resident(shape):
    return pl.BlockSpec(shape, lambda bi, si: (0,) * len(shape), pipeline_mode=pl.Buffered(1))


def _inproj_call(x, pos_f, norm_g, sc_mix, sh_mix, freq_row, bf_row, qg, kvg, weights, tm):
    b, s, d = x.shape
    tok = lambda w: pl.BlockSpec((1, tm, w), lambda bi, si: (bi, si, 0))
    vec = pl.BlockSpec((1, 1, d), lambda bi, si: (bi, 0, 0))
    return pl.pallas_call(
        functools.partial(_inproj_kernel, tm=tm),
        grid=(b, s // tm),
        in_specs=[tok(d), pl.BlockSpec((1, 1, tm), lambda bi, si: (bi, 0, si)),
                  _resident((1, d)), vec, vec, _resident((1, LANES)),
                  _resident((1, LANES)), _resident((1, MLA_Q_RANK)), _resident((1, MLA_KV_RANK))]
                 + [_resident(w.shape) for w in weights],
        out_specs=[pl.BlockSpec((1, N_HEADS * HEAD_PAD, tm), lambda bi, si: (bi, 0, si)),
                   tok(N_HEADS * HEAD_PAD),
                   pl.BlockSpec((1, N_HEADS * FOX_DIM, tm), lambda bi, si: (bi, 0, si)),
                   tok(2 * d)],
        out_shape=[jax.ShapeDtypeStruct((b, N_HEADS * HEAD_PAD, s), BF16),
                   jax.ShapeDtypeStruct((b, s, N_HEADS * HEAD_PAD), BF16),
                   jax.ShapeDtypeStruct((b, N_HEADS * FOX_DIM, s), BF16),
                   jax.ShapeDtypeStruct((b, s, 2 * d), BF16)],
        scratch_shapes=[pltpu.VMEM((1, LANES), F32)],
        compiler_params=pltpu.CompilerParams(dimension_semantics=("arbitrary", "arbitrary"),
                                             vmem_limit_bytes=VMEM_LIMIT),
        name="inproj",
    )(x, pos_f, norm_g, sc_mix, sh_mix, freq_row, bf_row, qg, kvg, *weights)


def _attn_kernel(qt_ref, k_ref, vt_ref, o_ref, st_ref, mx_ref, m_ref, acc_ref, *, tq, tk, nsub):
    qi = pl.program_id(2)
    assert tq == tk
    row = lax.broadcasted_iota(jnp.int32, (tk, tq), 0)
    col = lax.broadcasted_iota(jnp.int32, (tk, tq), 1)
    causal = row <= col
    ones_rows = jnp.ones((ONES_ROWS, tk), BF16)
    streams = [(hh, sub) for sub in range(nsub) for hh in range(2)]

    def scores(idx, j, slot, full):
        hh, sub = streams[idx]
        start = pl.multiple_of(j * tk, tk)
        qt = qt_ref[0, hh * HEAD_PAD:(hh + 1) * HEAD_PAD, sub * tq:(sub + 1) * tq]
        k = k_ref[0, pl.ds(start, tk), hh * HEAD_PAD:(hh + 1) * HEAD_PAD]
        st = jnp.dot(k, qt, preferred_element_type=F32)
        if full is False:
            st = jnp.where(causal, st, -jnp.inf)
        elif full is not True:
            st = jnp.where(row <= col + jnp.where(full, tk, 0), st, -jnp.inf)
        st_ref[slot, idx] = st
        mx_ref[slot, idx] = jnp.max(st, axis=0, keepdims=True)

    def update(idx, j, slot):
        hh, _ = streams[idx]
        start = pl.multiple_of(j * tk, tk)
        m = m_ref[idx]
        m_new = jnp.maximum(m, mx_ref[slot, idx])
        m_ref[idx] = m_new
        p = jnp.exp2(st_ref[slot, idx] - m_new).astype(BF16)
        vt = vt_ref[0, hh * MLA_V:(hh + 1) * MLA_V, pl.ds(start, tk)]
        vt_aug = jnp.concatenate([vt, ones_rows], axis=0)
        acc_ref[idx] = (jnp.exp2(m - m_new) * acc_ref[idx]
                        + jnp.dot(vt_aug, p, preferred_element_type=F32))

    def step(j, slot, diag_sub=None, next_is_full=None):
        first = 0 if diag_sub is None else diag_sub
        for idx, (_, sub) in enumerate(streams):
            if diag_sub is None:
                scores(idx, j + 1, 1 - slot, full=next_is_full if sub == 0 else True)
            elif sub > first:
                scores(idx, j + 1, 1 - slot, full=(sub != diag_sub + 1))
            if sub >= first:
                update(idx, j, slot)

    m_ref[...] = jnp.full(m_ref.shape, -jnp.inf, F32)
    acc_ref[...] = jnp.zeros(acc_ref.shape, F32)
    assert nsub % 2 == 0
    n_full = qi * nsub
    for idx, (_, sub) in enumerate(streams):
        scores(idx, 0, 0, full=(qi > 0) if sub == 0 else True)

    def body(i, carry):
        for u in range(nsub):
            step(nsub * i + u, u % 2, next_is_full=True if u < nsub - 1 else i < qi - 1)
        return carry

    lax.fori_loop(0, qi, body, 0)
    for r in range(nsub):
        step(n_full + r, r % 2, diag_sub=r)
        outs = []
        for hh in range(2):
            acc = acc_ref[streams.index((hh, r))]
            outs.append(acc[0:MLA_V, :] / acc[MLA_V:MLA_V + 1, :])
        o_ref[0, r * tq:(r + 1) * tq, :] = jnp.concatenate(outs, axis=0).T.astype(BF16)


def _attn_call(qt_all, k_all, vt_all, tq, nsub):
    b, s, _ = k_all.shape
    pairs = N_HEADS // 2
    bq = tq * nsub
    return pl.pallas_call(
        functools.partial(_attn_kernel, tq=tq, tk=tq, nsub=nsub),
        grid=(b, pairs, s // bq),
        in_specs=[pl.BlockSpec((1, 2 * HEAD_PAD, bq), lambda bi, pi, qi: (bi, pi, qi)),
                  pl.BlockSpec((1, s, 2 * HEAD_PAD), lambda bi, pi, qi: (bi, 0, pi)),
                  pl.BlockSpec((1, LANES, s), lambda bi, pi, qi: (bi, pi, 0))],
        out_specs=pl.BlockSpec((1, bq, LANES), lambda bi, pi, qi: (bi, qi, pi)),
        out_shape=jax.ShapeDtypeStruct((b, s, pairs * LANES), BF16),
        scratch_shapes=[pltpu.VMEM((2, 2 * nsub, tq, tq), F32),
                        pltpu.VMEM((2, 2 * nsub, 1, tq), F32),
                        pltpu.VMEM((2 * nsub, 1, tq), F32),
                        pltpu.VMEM((2 * nsub, MLA_V + ONES_ROWS, tq), F32)],
        compiler_params=pltpu.CompilerParams(
            dimension_semantics=("arbitrary", "arbitrary", "arbitrary"),
            vmem_limit_bytes=VMEM_LIMIT),
        name="attn",
    )(qt_all, k_all, vt_all)


def _outproj_kernel(o_ref, g_ref, x_ref, gm_ref, wm_ref, wf_ref, wo_ref, x1_ref):
    half = MLA_HEADS * MLA_V
    o = o_ref[0]
    g = g_ref[0].astype(F32)
    y = (g[:, :D_MODEL] * jnp.dot(o[:, :half], wm_ref[...], preferred_element_type=F32)
         + g[:, D_MODEL:] * jnp.dot(o[:, half:], wf_ref[...], preferred_element_type=F32))
    mix = jnp.dot(y.astype(BF16), wo_ref[...], preferred_element_type=F32)
    x1_ref[0] = x_ref[0] + gm_ref[0] * mix


def _outproj_call(o_all, g_all, x, g_m, wm, wf, wo, tm):
    b, s, d = x.shape
    const = _resident
    tok = lambda w: pl.BlockSpec((1, tm, w), lambda bi, si: (bi, si, 0))
    vec = pl.BlockSpec((1, 1, d), lambda bi, si: (bi, 0, 0))
    return pl.pallas_call(
        _outproj_kernel,
        grid=(b, s // tm),
        in_specs=[tok(o_all.shape[-1]), tok(g_all.shape[-1]), tok(d), vec,
                  const(wm.shape), const(wf.shape), const(wo.shape)],
        out_specs=tok(d),
        out_shape=jax.ShapeDtypeStruct((b, s, d), F32),
        compiler_params=pltpu.CompilerParams(dimension_semantics=("arbitrary", "arbitrary"),
                                             vmem_limit_bytes=VMEM_LIMIT),
        name="outproj",
    )(o_all, g_all, x, g_m, wm, wf, wo)


def _shift_rows(u, prev, k):
    tm = u.shape[0]
    body = pltpu.roll(u, k, 0)
    head = jnp.where(lax.broadcasted_iota(jnp.int32, prev.shape, 0) < k,
                     pltpu.roll(prev, k, 0), body[0:8, :])
    return jnp.concatenate([head, body[8:tm, :]], axis=0)


def _ffn_kernel(x1_ref, ng_ref, sc_ref, sh_ref, gf_ref, cw_ref, cb_ref, fg_ref, wup_ref, wdn_ref,
                out_ref, tail_ref, act_ref, *, chunk):
    s_idx = pl.program_id(1)

    @pl.when(s_idx == 0)
    def _():
        tail_ref[...] = jnp.zeros_like(tail_ref)

    x1 = x1_ref[0]
    tm = x1.shape[0]
    h = _modulated_norm(x1, ng_ref, sc_ref, sh_ref)

    def conv(c0):
        u = jnp.dot(h, wup_ref[:, c0:c0 + chunk], preferred_element_type=F32)
        prev = tail_ref[:, c0:c0 + chunk]
        tail_ref[:, c0:c0 + chunk] = u[tm - 8:tm, :]
        w = cw_ref[:, c0:c0 + chunk]
        return (cb_ref[:, c0:c0 + chunk] + w[0:1] * _shift_rows(u, prev, 2)
                + w[1:2] * _shift_rows(u, prev, 1) + w[2:3] * u)

    for c0 in range(0, D_FF, chunk):
        act_ref[:, c0:c0 + chunk] = (_silu(conv(c0)) * conv(D_FF + c0)).astype(BF16)

    y = jnp.dot(act_ref[...], wdn_ref[...], preferred_element_type=F32)
    x2 = x1 + gf_ref[0] * y
    out_ref[0] = _rms_scale(x2) * fg_ref[...]


def _ffn_call(x1, norm_g, sc_ffn, sh_ffn, g_f, conv_w, conv_b, final_g, wup, wdn, tm, chunk):
    b, s, d = x1.shape
    const = _resident
    tok = pl.BlockSpec((1, tm, d), lambda bi, si: (bi, si, 0))
    vec = pl.BlockSpec((1, 1, d), lambda bi, si: (bi, 0, 0))
    return pl.pallas_call(
        functools.partial(_ffn_kernel, chunk=chunk),
        grid=(b, s // tm),
        in_specs=[tok, const((1, d)), vec, vec, vec, const(conv_w.shape), const(conv_b.shape),
                  const(final_g.shape), const(wup.shape), const(wdn.shape)],
        out_specs=tok,
        out_shape=jax.ShapeDtypeStruct((b, s, d), F32),
        scratch_shapes=[pltpu.VMEM((8, 2 * D_FF), F32), pltpu.VMEM((tm, D_FF), BF16)],
        compiler_params=pltpu.CompilerParams(dimension_semantics=("arbitrary", "arbitrary"),
                                             vmem_limit_bytes=VMEM_LIMIT),
        name="ffn",
    )(x1, norm_g, sc_ffn, sh_ffn, g_f, conv_w, conv_b, final_g, wup, wdn)


def _pad_heads(w, heads, width):
    k = w.shape[0]
    w = w.reshape(k, heads, width)
    return jnp.pad(w, ((0, 0), (0, 0), (0, HEAD_PAD - width))).reshape(k, heads * HEAD_PAD)


def _prep_w_in(w_in):
    d = w_in.shape[0]
    splits = (MLA_Q_RANK, MLA_KV_RANK, MLA_ROPE, FOX_HEADS * FOX_DIM, FOX_HEADS * FOX_DIM,
              FOX_HEADS * FOX_DIM, FOX_HEADS, D_MODEL, D_MODEL)
    offs = [0]
    for n in splits:
        offs.append(offs[-1] + n)
    w_in = w_in.astype(BF16)
    piece = lambda i, j=None: w_in[:, offs[i]:offs[i + 1 if j is None else j]]
    misc = jnp.concatenate([jnp.zeros((d, MLA_NOPE), BF16), piece(2), piece(6),
                            jnp.zeros((d, LANES - FLOGIT_LO - FOX_HEADS), BF16)], axis=1)
    return [piece(0, 2), misc, _pad_heads(piece(3), FOX_HEADS, FOX_DIM),
            _pad_heads(piece(4), FOX_HEADS, FOX_DIM), piece(5), piece(7, 9)]


def _bias_placement():
    part, head = np.meshgrid(np.arange(3), np.arange(FOX_HEADS), indexing="ij")
    rows = (FLOGIT_LO + part * FOX_HEADS + head).reshape(-1)
    q_cols = (head * HEAD_PAD + AUG_LO + part).reshape(-1)
    k_cols = (head * HEAD_PAD + AUG_LO + 3 + part).reshape(-1)
    width = FOX_HEADS * HEAD_PAD
    sq, sk = np.zeros((LANES, width), np.float32), np.zeros((LANES, width), np.float32)
    qc, kc = np.zeros((1, width), np.float32), np.zeros((1, width), np.float32)
    sq[rows, q_cols] = 1.0
    sk[rows, k_cols] = -1.0
    qc[0, k_cols] = 1.0
    kc[0, q_cols] = 1.0
    return jnp.asarray(sq, BF16), jnp.asarray(sk, BF16), jnp.asarray(qc), jnp.asarray(kc)


def kernel(x, c, positions, w_ada, b_ada, norm_mix_g, w_in, q_norm_g, w_uq, kv_norm_g, w_ukv,
           b_forget, w_o_mla, w_o_fox, w_out, norm_ffn_g, w_up, conv_w, conv_b, w_down,
           norm_final_g):
    assert w_ada.shape[0] == 1, "single-layer block"
    b, s, d = x.shape
    ada = _ada_call(c, w_ada[0], b_ada[0])
    sh_m, sc_m, g_m, sh_f, sc_f, g_f = (ada[:, i * d:(i + 1) * d].reshape(b, 1, d)
                                        for i in range(N_ADA))

    inv_freq = ROPE_THETA ** (-jnp.arange(0, MLA_ROPE, 2, dtype=F32) / MLA_ROPE)
    freq_row = jnp.concatenate([jnp.zeros((MLA_NOPE,), F32), inv_freq, inv_freq,
                                jnp.zeros((LANES - FLOGIT_LO,), F32)]).reshape(1, LANES)
    bf_row = jnp.concatenate([jnp.zeros((FLOGIT_LO,), F32), b_forget[0].astype(F32),
                              jnp.zeros((LANES - FLOGIT_LO - FOX_HEADS,), F32)]).reshape(1, LANES)
    pos_f = positions.astype(F32).reshape(b, 1, s)

    wuq = _pad_heads(w_uq[0], MLA_HEADS, MLA_NOPE + MLA_ROPE).astype(BF16)
    wukv = w_ukv[0].reshape(MLA_KV_RANK, MLA_HEADS, MLA_NOPE + MLA_V)
    wukvk = _pad_heads(wukv[:, :, :MLA_NOPE].reshape(MLA_KV_RANK, -1), MLA_HEADS,
                       MLA_NOPE).astype(BF16)
    wukvv = wukv[:, :, MLA_NOPE:].reshape(MLA_KV_RANK, -1).astype(BF16)

    qt_all, k_all, vt_all, g_all = _inproj_call(
        x, pos_f, norm_mix_g[0].reshape(1, d), sc_m, sh_m, freq_row, bf_row,
        q_norm_g[0].reshape(1, -1), kv_norm_g[0].reshape(1, -1),
        _prep_w_in(w_in[0]) + [wuq, wukvk, wukvv] + list(_bias_placement()), tm=512)
    o_all = _attn_call(qt_all, k_all, vt_all, tq=256, nsub=8)
    x1 = _outproj_call(o_all, g_all, x, g_m, w_o_mla[0].astype(BF16), w_o_fox[0].astype(BF16),
                       w_out[0].astype(BF16), tm=512)
    return _ffn_call(x1, norm_ffn_g[0].reshape(1, d), sc_f, sh_f, g_f, conv_w[0],
                     conv_b[0].reshape(1, -1),
                     norm_final_g.reshape(1, -1), w_up[0].astype(BF16), w_down[0].astype(BF16),
                     tm=512, chunk=256)
```

```python
import functools
import math

import jax
import jax.numpy as jnp
import numpy as np
from jax import lax
from jax.experimental import pallas as pl
from jax.experimental.pallas import tpu as pltpu

F32 = jnp.float32
BF16 = jnp.bfloat16

D_MODEL = 1024
MLA_HEADS = 8
MLA_Q_RANK = 384
MLA_KV_RANK = 256
MLA_NOPE = 64
MLA_ROPE = 32
MLA_V = 64
MLA_SCALE = 1.0 / math.sqrt(MLA_NOPE + MLA_ROPE)
ROPE_THETA = 10000.0
FOX_HEADS = 8
FOX_DIM = 64
FOX_SCALE = 1.0 / math.sqrt(FOX_DIM)
LOG2E = math.log2(math.e)
D_FF = 2816
CONV_WIDTH = 3
EPS = 1e-6
N_ADA = 6

LANES = 128
HEAD_PAD = LANES
N_HEADS = MLA_HEADS + FOX_HEADS
ROPE_HALF = MLA_ROPE // 2
ROPE_LO = MLA_NOPE
ROPE_HI = MLA_NOPE + ROPE_HALF
FLOGIT_LO = MLA_NOPE + MLA_ROPE
AUG_LO = FOX_DIM
ONES_ROWS = 16

VMEM_LIMIT = 56 * 1024 * 1024


def _silu(x):
    return x * (1.0 / (1.0 + jnp.exp(-x)))


def _sigmoid(x):
    return 1.0 / (1.0 + jnp.exp(-x))


def _rms_scale(x):
    return x * lax.rsqrt(jnp.mean(x * x, axis=-1, keepdims=True) + EPS)


def _ada_kernel(ct_ref, w_ref, b_ref, o_ref, *, batch):
    act = _silu(ct_ref[...])
    w = w_ref[...]
    for b in range(batch):
        o_ref[b:b + 1, :] = jnp.sum(w * act[:, b:b + 1], axis=0, keepdims=True) + b_ref[...]


def _ada_call(c, w_ada, b_ada):
    batch, d = c.shape
    n = w_ada.shape[1]
    tn = 1536
    return pl.pallas_call(
        functools.partial(_ada_kernel, batch=batch),
        grid=(n // tn,),
        in_specs=[pl.BlockSpec((d, batch), lambda j: (0, 0)),
                  pl.BlockSpec((d, tn), lambda j: (0, j)),
                  pl.BlockSpec((1, tn), lambda j: (0, j))],
        out_specs=pl.BlockSpec((batch, tn), lambda j: (0, j)),
        out_shape=jax.ShapeDtypeStruct((batch, n), F32),
        compiler_params=pltpu.CompilerParams(dimension_semantics=("arbitrary",),
                                             vmem_limit_bytes=VMEM_LIMIT),
        name="ada",
    )(c.T, w_ada, b_ada.reshape(1, n))


def _rope_tables(pos_col, freq_row):
    ang = pos_col * freq_row
    lane = lax.broadcasted_iota(jnp.int32, ang.shape, 1)
    cos = jnp.cos(ang)
    sin = jnp.sin(ang)
    s_lo = jnp.where((lane >= ROPE_LO) & (lane < ROPE_HI), -sin, 0.0)
    s_hi = jnp.where((lane >= ROPE_HI) & (lane < FLOGIT_LO), sin, 0.0)
    return cos, s_lo, s_hi


def _rope_block(t, cos, s_lo, s_hi):
    return (t * cos + pltpu.roll(t, LANES - ROPE_HALF, 1) * s_lo
            + pltpu.roll(t, ROPE_HALF, 1) * s_hi)


def _split3(x):
    hi = x.astype(BF16)
    r1 = x - hi.astype(F32)
    mid = r1.astype(BF16)
    lo = (r1 - mid.astype(F32)).astype(BF16)
    return hi, mid, lo


def _pack3(x):
    hi, mid, lo = (p.astype(F32) for p in _split3(x))
    return (hi + pltpu.roll(mid, FOX_HEADS, 1) + pltpu.roll(lo, 2 * FOX_HEADS, 1)).astype(BF16)


def _modulated_norm(x, gain_ref, sc_ref, sh_ref):
    return (_rms_scale(x) * (gain_ref[...] * (1.0 + sc_ref[0])) + sh_ref[0]).astype(BF16)


def _inproj_kernel(x_ref, pos_ref, ng_ref, sc_ref, sh_ref, freq_ref, bf_ref, qg_ref, kvg_ref,
                   wlat_ref, wfox_ref, wgate_ref,
                   wuq_ref, wukv_ref, sq_ref, sk_ref, qc_ref, kc_ref,
                   qt_ref, k_ref, vt_ref, g_ref, carry_ref, *, tm):
    s_idx = pl.program_id(1)

    @pl.when(s_idx == 0)
    def _():
        carry_ref[...] = jnp.zeros_like(carry_ref)

    h = _modulated_norm(x_ref[0], ng_ref, sc_ref, sh_ref)

    def proj(w_ref, lo=None, hi=None):
        w = w_ref[...] if lo is None else w_ref[:, lo:hi]
        return jnp.dot(h, w, preferred_element_type=F32)

    lane = lax.broadcasted_iota(jnp.int32, (tm, LANES), 1)
    pos_col = jnp.broadcast_to(pos_ref[0], (8, tm)).T[:, 0:1]
    cos, s_lo, s_hi = _rope_tables(pos_col, freq_ref[...])

    lat = proj(wlat_ref)
    lat_q = lat[:, 0:MLA_Q_RANK]
    lat_kv = lat[:, MLA_Q_RANK:MLA_Q_RANK + MLA_KV_RANK]
    misc = lat[:, MLA_Q_RANK + MLA_KV_RANK:]
    g_ref[0, :, 0:D_MODEL] = _sigmoid(proj(wgate_ref, 0, D_MODEL)).astype(BF16)

    cqn = (_rms_scale(lat_q) * qg_ref[...]).astype(BF16)
    ckvn = (_rms_scale(lat_kv) * kvg_ref[...]).astype(BF16)
    qm = jnp.dot(cqn, wuq_ref[...], preferred_element_type=F32)
    kvm = jnp.dot(ckvn, wukv_ref[...], preferred_element_type=F32)
    kn, vm = kvm[:, 0:MLA_HEADS * MLA_NOPE], kvm[:, MLA_HEADS * MLA_NOPE:]
    fv = proj(wfox_ref, 2 * FOX_HEADS * FOX_DIM, 3 * FOX_HEADS * FOX_DIM)
    for hd in range(MLA_HEADS):
        blk = qm[:, hd * HEAD_PAD:(hd + 1) * HEAD_PAD]
        qt_ref[0, hd * HEAD_PAD:(hd + 1) * HEAD_PAD, :] = (
            _rope_block(blk, cos, s_lo, s_hi) * (MLA_SCALE * LOG2E)).T.astype(BF16)

    krope = _rope_block(jnp.where(lane < FLOGIT_LO, misc, 0.0), cos, s_lo, s_hi)
    for hd in range(MLA_HEADS):
        kb = kn[:, (hd // 2) * LANES:(hd // 2 + 1) * LANES]
        if hd % 2:
            kb = pltpu.roll(kb, MLA_NOPE, 1)
        k_ref[0, :, hd * HEAD_PAD:(hd + 1) * HEAD_PAD] = jnp.where(
            lane < MLA_NOPE, kb, krope).astype(BF16)
    vt_ref[0, 0:MLA_HEADS * MLA_V, :] = vm.T.astype(BF16)
    vt_ref[0, MLA_HEADS * MLA_V:, :] = fv.T.astype(BF16)

    z = misc + bf_ref[...]
    logf = jnp.minimum(z, 0.0) - jnp.log1p(jnp.exp(-jnp.abs(z)))
    logf = jnp.where((lane >= FLOGIT_LO) & (lane < FLOGIT_LO + FOX_HEADS), logf, 0.0)
    row = lax.broadcasted_iota(jnp.int32, (tm, tm), 0)
    col = lax.broadcasted_iota(jnp.int32, (tm, tm), 1)
    tri = jnp.where(col <= row, 1.0, 0.0).astype(BF16)
    parts = jnp.dot(tri, _pack3(logf), preferred_element_type=F32)
    total = (parts + pltpu.roll(parts, LANES - FOX_HEADS, 1)
             + pltpu.roll(parts, LANES - 2 * FOX_HEADS, 1))
    fcum = carry_ref[...] + jnp.where(
        (lane >= FLOGIT_LO) & (lane < FLOGIT_LO + FOX_HEADS), total, 0.0)
    carry_ref[...] = fcum[tm - 1:tm, :]
    fq = proj(wfox_ref, 0, FOX_HEADS * FOX_DIM) * (FOX_SCALE * LOG2E)
    fk = proj(wfox_ref, FOX_HEADS * FOX_DIM, 2 * FOX_HEADS * FOX_DIM)

    bias3 = _pack3(fcum * LOG2E)
    aug_q = jnp.dot(bias3, sq_ref[...], preferred_element_type=F32) + qc_ref[...]
    aug_k = jnp.dot(bias3, sk_ref[...], preferred_element_type=F32) + kc_ref[...]
    for hd in range(FOX_HEADS):
        pair = slice((hd // 2) * LANES, (hd // 2 + 1) * LANES)
        blk = slice(hd * HEAD_PAD, (hd + 1) * HEAD_PAD)
        qb, kb = fq[:, pair], fk[:, pair]
        if hd % 2:
            qb, kb = pltpu.roll(qb, FOX_DIM, 1), pltpu.roll(kb, FOX_DIM, 1)
        c0 = (MLA_HEADS + hd) * HEAD_PAD
        qt_ref[0, c0:c0 + HEAD_PAD, :] = jnp.where(lane < FOX_DIM, qb, aug_q[:, blk]).T.astype(BF16)
        k_ref[0, :, c0:c0 + HEAD_PAD] = jnp.where(lane < FOX_DIM, kb, aug_k[:, blk]).astype(BF16)
    g_ref[0, :, D_MODEL:] = _sigmoid(proj(wgate_ref, D_MODEL, 2 * D_MODEL)).astype(BF16)


def _resident(shape):
    return pl.BlockSpec(shape, lambda bi, si: (0,) * len(shape), pipeline_mode=pl.Buffered(1))


def _inproj_call(x, pos_f, norm_g, sc_mix, sh_mix, freq_row, bf_row, qg, kvg, weights, tm):
    b, s, d = x.shape
    tok = lambda w: pl.BlockSpec((1, tm, w), lambda bi, si: (bi, si, 0))
    vec = pl.BlockSpec((1, 1, d), lambda bi, si: (bi, 0, 0))
    return pl.pallas_call(
        functools.partial(_inproj_kernel, tm=tm),
        grid=(b, s // tm),
        in_specs=[tok(d), pl.BlockSpec((1, 1, tm), lambda bi, si: (bi, 0, si)),
                  _resident((1, d)), vec, vec, _resident((1, LANES)),
                  _resident((1, LANES)), _resident((1, MLA_Q_RANK)), _resident((1, MLA_KV_RANK))]
                 + [_resident(w.shape) for w in weights],
        out_specs=[pl.BlockSpec((1, N_HEADS * HEAD_PAD, tm), lambda bi, si: (bi, 0, si)),
                   tok(N_HEADS * HEAD_PAD),
                   pl.BlockSpec((1, N_HEADS * FOX_DIM, tm), lambda bi, si: (bi, 0, si)),
                   tok(2 * d)],
        out_shape=[jax.ShapeDtypeStruct((b, N_HEADS * HEAD_PAD, s), BF16),
                   jax.ShapeDtypeStruct((b, s, N_HEADS * HEAD_PAD), BF16),
                   jax.ShapeDtypeStruct((b, N_HEADS * FOX_DIM, s), BF16),
                   jax.ShapeDtypeStruct((b, s, 2 * d), BF16)],
        scratch_shapes=[pltpu.VMEM((1, LANES), F32)],
        compiler_params=pltpu.CompilerParams(dimension_semantics=("arbitrary", "arbitrary"),
                                             vmem_limit_bytes=VMEM_LIMIT),
        name="inproj",
    )(x, pos_f, norm_g, sc_mix, sh_mix, freq_row, bf_row, qg, kvg, *weights)


def _attn_kernel(qt_ref, k_ref, vt_ref, o_ref, st_ref, mx_ref, m_ref, acc_ref, *, tq, tk, nsub):
    qi = pl.program_id(2)
    assert tq == tk
    row = lax.broadcasted_iota(jnp.int32, (tk, tq), 0)
    col = lax.broadcasted_iota(jnp.int32, (tk, tq), 1)
    causal = row <= col
    ones_rows = jnp.ones((ONES_ROWS, tk), BF16)
    streams = [(hh, sub) for sub in range(nsub) for hh in range(2)]

    def scores(idx, j, slot, full):
        hh, sub = streams[idx]
        start = pl.multiple_of(j * tk, tk)
        qt = qt_ref[0, hh * HEAD_PAD:(hh + 1) * HEAD_PAD, sub * tq:(sub + 1) * tq]
        k = k_ref[0, pl.ds(start, tk), hh * HEAD_PAD:(hh + 1) * HEAD_PAD]
        st = jnp.dot(k, qt, preferred_element_type=F32)
        if full is False:
            st = jnp.where(causal, st, -jnp.inf)
        elif full is not True:
            st = jnp.where(row <= col + jnp.where(full, tk, 0), st, -jnp.inf)
        st_ref[slot, idx] = st
        mx_ref[slot, idx] = jnp.max(st, axis=0, keepdims=True)

    def update(idx, j, slot):
        hh, _ = streams[idx]
        start = pl.multiple_of(j * tk, tk)
        m = m_ref[idx]
        m_new = jnp.maximum(m, mx_ref[slot, idx])
        m_ref[idx] = m_new
        p = jnp.exp2(st_ref[slot, idx] - m_new).astype(BF16)
        vt = vt_ref[0, hh * MLA_V:(hh + 1) * MLA_V, pl.ds(start, tk)]
        vt_aug = jnp.concatenate([vt, ones_rows], axis=0)
        acc_ref[idx] = (jnp.exp2(m - m_new) * acc_ref[idx]
                        + jnp.dot(vt_aug, p, preferred_element_type=F32))

    def step(j, slot, diag_sub=None, next_is_full=None):
        first = 0 if diag_sub is None else diag_sub
        for idx, (_, sub) in enumerate(streams):
            if diag_sub is None:
                scores(idx, j + 1, 1 - slot, full=next_is_full if sub == 0 else True)
            elif sub > first:
                scores(idx, j + 1, 1 - slot, full=(sub != diag_sub + 1))
            if sub >= first:
                update(idx, j, slot)

    m_ref[...] = jnp.full(m_ref.shape, -jnp.inf, F32)
    acc_ref[...] = jnp.zeros(acc_ref.shape, F32)
    assert nsub % 2 == 0
    n_full = qi * nsub
    for idx, (_, sub) in enumerate(streams):
        scores(idx, 0, 0, full=(qi > 0) if sub == 0 else True)

    def body(i, carry):
        for u in range(nsub):
            step(nsub * i + u, u % 2, next_is_full=True if u < nsub - 1 else i < qi - 1)
        return carry

    lax.fori_loop(0, qi, body, 0)
    for r in range(nsub):
        step(n_full + r, r % 2, diag_sub=r)
        outs = []
        for hh in range(2):
            acc = acc_ref[streams.index((hh, r))]
            outs.append(acc[0:MLA_V, :] / acc[MLA_V:MLA_V + 1, :])
        o_ref[0, r * tq:(r + 1) * tq, :] = jnp.concatenate(outs, axis=0).T.astype(BF16)


def _attn_call(qt_all, k_all, vt_all, tq, nsub):
    b, s, _ = k_all.shape
    pairs = N_HEADS // 2
    bq = tq * nsub
    return pl.pallas_call(
        functools.partial(_attn_kernel, tq=tq, tk=tq, nsub=nsub),
        grid=(b, pairs, s // bq),
        in_specs=[pl.BlockSpec((1, 2 * HEAD_PAD, bq), lambda bi, pi, qi: (bi, pi, qi)),
                  pl.BlockSpec((1, s, 2 * HEAD_PAD), lambda bi, pi, qi: (bi, 0, pi)),
                  pl.BlockSpec((1, LANES, s), lambda bi, pi, qi: (bi, pi, 0))],
        out_specs=pl.BlockSpec((1, bq, LANES), lambda bi, pi, qi: (bi, qi, pi)),
        out_shape=jax.ShapeDtypeStruct((b, s, pairs * LANES), BF16),
        scratch_shapes=[pltpu.VMEM((2, 2 * nsub, tq, tq), F32),
                        pltpu.VMEM((2, 2 * nsub, 1, tq), F32),
                        pltpu.VMEM((2 * nsub, 1, tq), F32),
                        pltpu.VMEM((2 * nsub, MLA_V + ONES_ROWS, tq), F32)],
        compiler_params=pltpu.CompilerParams(
            dimension_semantics=("arbitrary", "arbitrary", "arbitrary"),
            vmem_limit_bytes=VMEM_LIMIT),
        name="attn",
    )(qt_all, k_all, vt_all)


def _outproj_kernel(o_ref, g_ref, x_ref, gm_ref, wm_ref, wf_ref, wo_ref, x1_ref):
    half = MLA_HEADS * MLA_V
    o = o_ref[0]
    g = g_ref[0].astype(F32)
    y = (g[:, :D_MODEL] * jnp.dot(o[:, :half], wm_ref[...], preferred_element_type=F32)
         + g[:, D_MODEL:] * jnp.dot(o[:, half:], wf_ref[...], preferred_element_type=F32))
    mix = jnp.dot(y.astype(BF16), wo_ref[...], preferred_element_type=F32)
    x1_ref[0] = x_ref[0] + gm_ref[0] * mix


def _outproj_call(o_all, g_all, x, g_m, wm, wf, wo, tm):
    b, s, d = x.shape
    const = _resident
    tok = lambda w: pl.BlockSpec((1, tm, w), lambda bi, si: (bi, si, 0))
    vec = pl.BlockSpec((1, 1, d), lambda bi, si: (bi, 0, 0))
    return pl.pallas_call(
        _outproj_kernel,
        grid=(b, s // tm),
        in_specs=[tok(o_all.shape[-1]), tok(g_all.shape[-1]), tok(d), vec,
                  const(wm.shape), const(wf.shape), const(wo.shape)],
        out_specs=tok(d),
        out_shape=jax.ShapeDtypeStruct((b, s, d), F32),
        compiler_params=pltpu.CompilerParams(dimension_semantics=("arbitrary", "arbitrary"),
                                             vmem_limit_bytes=VMEM_LIMIT),
        name="outproj",
    )(o_all, g_all, x, g_m, wm, wf, wo)


def _shift_rows(u, prev, k):
    tm = u.shape[0]
    body = pltpu.roll(u, k, 0)
    head = jnp.where(lax.broadcasted_iota(jnp.int32, prev.shape, 0) < k,
                     pltpu.roll(prev, k, 0), body[0:8, :])
    return jnp.concatenate([head, body[8:tm, :]], axis=0)


def _ffn_kernel(x1_ref, ng_ref, sc_ref, sh_ref, gf_ref, cw_ref, cb_ref, fg_ref, wup_ref, wdn_ref,
                out_ref, tail_ref, act_ref, *, chunk):
    s_idx = pl.program_id(1)

    @pl.when(s_idx == 0)
    def _():
        tail_ref[...] = jnp.zeros_like(tail_ref)

    x1 = x1_ref[0]
    tm = x1.shape[0]
    h = _modulated_norm(x1, ng_ref, sc_ref, sh_ref)

    def conv(c0):
        u = jnp.dot(h, wup_ref[:, c0:c0 + chunk], preferred_element_type=F32)
        prev = tail_ref[:, c0:c0 + chunk]
        tail_ref[:, c0:c0 + chunk] = u[tm - 8:tm, :]
        w = cw_ref[:, c0:c0 + chunk]
        return (cb_ref[:, c0:c0 + chunk] + w[0:1] * _shift_rows(u, prev, 2)
                + w[1:2] * _shift_rows(u, prev, 1) + w[2:3] * u)

    for c0 in range(0, D_FF, chunk):
        act_ref[:, c0:c0 + chunk] = (_silu(conv(c0)) * conv(D_FF + c0)).astype(BF16)

    y = jnp.dot(act_ref[...], wdn_ref[...], preferred_element_type=F32)
    x2 = x1 + gf_ref[0] * y
    out_ref[0] = _rms_scale(x2) * fg_ref[...]


def _ffn_call(x1, norm_g, sc_ffn, sh_ffn, g_f, conv_w, conv_b, final_g, wup, wdn, tm, chunk):
    b, s, d = x1.shape
    const = _resident
    tok = pl.BlockSpec((1, tm, d), lambda bi, si: (bi, si, 0))
    vec = pl.BlockSpec((1, 1, d), lambda bi, si: (bi, 0, 0))
    return pl.pallas_call(
        functools.partial(_ffn_kernel, chunk=chunk),
        grid=(b, s // tm),
        in_specs=[tok, const((1, d)), vec, vec, vec, const(conv_w.shape), const(conv_b.shape),
                  const(final_g.shape), const(wup.shape), const(wdn.shape)],
        out_specs=tok,
        out_shape=jax.ShapeDtypeStruct((b, s, d), F32),
        scratch_shapes=[pltpu.VMEM((8, 2 * D_FF), F32), pltpu.VMEM((tm, D_FF), BF16)],
        compiler_params=pltpu.CompilerParams(dimension_semantics=("arbitrary", "arbitrary"),
                                             vmem_limit_bytes=VMEM_LIMIT),
        name="ffn",
    )(x1, norm_g, sc_ffn, sh_ffn, g_f, conv_w, conv_b, final_g, wup, wdn)


def _pad_heads(w, heads, width):
    k = w.shape[0]
    w = w.reshape(k, heads, width)
    return jnp.pad(w, ((0, 0), (0, 0), (0, HEAD_PAD - width))).reshape(k, heads * HEAD_PAD)


def _prep_w_in(w_in):
    d = w_in.shape[0]
    splits = (MLA_Q_RANK, MLA_KV_RANK, MLA_ROPE, FOX_HEADS * FOX_DIM, FOX_HEADS * FOX_DIM,
              FOX_HEADS * FOX_DIM, FOX_HEADS, D_MODEL, D_MODEL)
    offs = [0]
    for n in splits:
        offs.append(offs[-1] + n)
    w_in = w_in.astype(BF16)
    piece = lambda i, j=None: w_in[:, offs[i]:offs[i + 1 if j is None else j]]
    misc = jnp.concatenate([jnp.zeros((d, MLA_NOPE), BF16), piece(2), piece(6),
                            jnp.zeros((d, LANES - FLOGIT_LO - FOX_HEADS), BF16)], axis=1)
    return [jnp.concatenate([piece(0, 2), misc], axis=1), piece(3, 6), piece(7, 9)]


def _bias_placement():
    part, head = np.meshgrid(np.arange(3), np.arange(FOX_HEADS), indexing="ij")
    rows = (FLOGIT_LO + part * FOX_HEADS + head).reshape(-1)
    q_cols = (head * HEAD_PAD + AUG_LO + part).reshape(-1)
    k_cols = (head * HEAD_PAD + AUG_LO + 3 + part).reshape(-1)
    width = FOX_HEADS * HEAD_PAD
    sq, sk = np.zeros((LANES, width), np.float32), np.zeros((LANES, width), np.float32)
    qc, kc = np.zeros((1, width), np.float32), np.zeros((1, width), np.float32)
    sq[rows, q_cols] = 1.0
    sk[rows, k_cols] = -1.0
    qc[0, k_cols] = 1.0
    kc[0, q_cols] = 1.0
    return jnp.asarray(sq, BF16), jnp.asarray(sk, BF16), jnp.asarray(qc), jnp.asarray(kc)


def kernel(x, c, positions, w_ada, b_ada, norm_mix_g, w_in, q_norm_g, w_uq, kv_norm_g, w_ukv,
           b_forget, w_o_mla, w_o_fox, w_out, norm_ffn_g, w_up, conv_w, conv_b, w_down,
           norm_final_g):
    assert w_ada.shape[0] == 1, "single-layer block"
    b, s, d = x.shape
    ada = _ada_call(c, w_ada[0], b_ada[0])
    sh_m, sc_m, g_m, sh_f, sc_f, g_f = (ada[:, i * d:(i + 1) * d].reshape(b, 1, d)
                                        for i in range(N_ADA))

    inv_freq = ROPE_THETA ** (-jnp.arange(0, MLA_ROPE, 2, dtype=F32) / MLA_ROPE)
    freq_row = jnp.concatenate([jnp.zeros((MLA_NOPE,), F32), inv_freq, inv_freq,
                                jnp.zeros((LANES - FLOGIT_LO,), F32)]).reshape(1, LANES)
    bf_row = jnp.concatenate([jnp.zeros((FLOGIT_LO,), F32), b_forget[0].astype(F32),
                              jnp.zeros((LANES - FLOGIT_LO - FOX_HEADS,), F32)]).reshape(1, LANES)
    pos_f = positions.astype(F32).reshape(b, 1, s)

    wuq = _pad_heads(w_uq[0], MLA_HEADS, MLA_NOPE + MLA_ROPE).astype(BF16)
    wukv = w_ukv[0].reshape(MLA_KV_RANK, MLA_HEADS, MLA_NOPE + MLA_V)
    wukv = jnp.concatenate([wukv[:, :, :MLA_NOPE].reshape(MLA_KV_RANK, -1),
                            wukv[:, :, MLA_NOPE:].reshape(MLA_KV_RANK, -1)], axis=1).astype(BF16)

    qt_all, k_all, vt_all, g_all = _inproj_call(
        x, pos_f, norm_mix_g[0].reshape(1, d), sc_m, sh_m, freq_row, bf_row,
        q_norm_g[0].reshape(1, -1), kv_norm_g[0].reshape(1, -1),
        _prep_w_in(w_in[0]) + [wuq, wukv] + list(_bias_placement()), tm=512)
    o_all = _attn_call(qt_all, k_all, vt_all, tq=256, nsub=8)
    x1 = _outproj_call(o_all, g_all, x, g_m, w_o_mla[0].astype(BF16), w_o_fox[0].astype(BF16),
                       w_out[0].astype(BF16), tm=512)
    return _ffn_call(x1, norm_ffn_g[0].reshape(1, d), sc_f, sh_f, g_f, conv_w[0],
                     conv_b[0].reshape(1, -1),
                     norm_final_g.reshape(1, -1), w_up[0].astype(BF16), w_down[0].astype(BF16),
                     tm=512, chunk=256)
```

```python
import functools
import math

import jax
import jax.numpy as jnp
import numpy as np
from jax import lax
from jax.experimental import pallas as pl
from jax.experimental.pallas import tpu as pltpu

F32 = jnp.float32
BF16 = jnp.bfloat16

D_MODEL = 1024
MLA_HEADS = 8
MLA_Q_RANK = 384
MLA_KV_RANK = 256
MLA_NOPE = 64
MLA_ROPE = 32
MLA_V = 64
MLA_SCALE = 1.0 / math.sqrt(MLA_NOPE + MLA_ROPE)
ROPE_THETA = 10000.0
FOX_HEADS = 8
FOX_DIM = 64
FOX_SCALE = 1.0 / math.sqrt(FOX_DIM)
LOG2E = math.log2(math.e)
D_FF = 2816
CONV_WIDTH = 3
EPS = 1e-6
N_ADA = 6

LANES = 128
HEAD_PAD = LANES
N_HEADS = MLA_HEADS + FOX_HEADS
ROPE_HALF = MLA_ROPE // 2
ROPE_LO = MLA_NOPE
ROPE_HI = MLA_NOPE + ROPE_HALF
FLOGIT_LO = MLA_NOPE + MLA_ROPE
AUG_LO = FOX_DIM
ONES_ROWS = 16

VMEM_LIMIT = 56 * 1024 * 1024


def _silu(x):
    return x * (1.0 / (1.0 + jnp.exp(-x)))


def _sigmoid(x):
    return 1.0 / (1.0 + jnp.exp(-x))


def _rms_scale(x):
    return x * lax.rsqrt(jnp.mean(x * x, axis=-1, keepdims=True) + EPS)


def _ada_kernel(ct_ref, w_ref, b_ref, o_ref, *, batch):
    act = _silu(ct_ref[...])
    w = w_ref[...]
    for b in range(batch):
        o_ref[b:b + 1, :] = jnp.sum(w * act[:, b:b + 1], axis=0, keepdims=True) + b_ref[...]


def _ada_call(c, w_ada, b_ada):
    batch, d = c.shape
    n = w_ada.shape[1]
    tn = 1536
    return pl.pallas_call(
        functools.partial(_ada_kernel, batch=batch),
        grid=(n // tn,),
        in_specs=[pl.BlockSpec((d, batch), lambda j: (0, 0)),
                  pl.BlockSpec((d, tn), lambda j: (0, j)),
                  pl.BlockSpec((1, tn), lambda j: (0, j))],
        out_specs=pl.BlockSpec((batch, tn), lambda j: (0, j)),
        out_shape=jax.ShapeDtypeStruct((batch, n), F32),
        compiler_params=pltpu.CompilerParams(dimension_semantics=("arbitrary",),
                                             vmem_limit_bytes=VMEM_LIMIT),
        name="ada",
    )(c.T, w_ada, b_ada.reshape(1, n))


def _rope_tables(pos_col, freq_row):
    ang = pos_col * freq_row
    lane = lax.broadcasted_iota(jnp.int32, ang.shape, 1)
    cos = jnp.cos(ang)
    sin = jnp.sin(ang)
    s_lo = jnp.where((lane >= ROPE_LO) & (lane < ROPE_HI), -sin, 0.0)
    s_hi = jnp.where((lane >= ROPE_HI) & (lane < FLOGIT_LO), sin, 0.0)
    return cos, s_lo, s_hi


def _rope_block(t, cos, s_lo, s_hi):
    return (t * cos + pltpu.roll(t, LANES - ROPE_HALF, 1) * s_lo
            + pltpu.roll(t, ROPE_HALF, 1) * s_hi)


def _split3(x):
    hi = x.astype(BF16)
    r1 = x - hi.astype(F32)
    mid = r1.astype(BF16)
    lo = (r1 - mid.astype(F32)).astype(BF16)
    return hi, mid, lo


def _pack3(x):
    hi, mid, lo = (p.astype(F32) for p in _split3(x))
    return (hi + pltpu.roll(mid, FOX_HEADS, 1) + pltpu.roll(lo, 2 * FOX_HEADS, 1)).astype(BF16)


def _modulated_norm(x, gain_ref, sc_ref, sh_ref):
    return (_rms_scale(x) * (gain_ref[...] * (1.0 + sc_ref[0])) + sh_ref[0]).astype(BF16)


def _inproj_kernel(x_ref, pos_ref, ng_ref, sc_ref, sh_ref, freq_ref, bf_ref, qg_ref, kvg_ref,
                   wlat_ref, wfox_ref, wgate_ref,
                   wuq_ref, wukv_ref, sq_ref, sk_ref, qc_ref, kc_ref,
                   qt_ref, k_ref, vt_ref, g_ref, carry_ref, *, tm):
    s_idx = pl.program_id(1)

    @pl.when(s_idx == 0)
    def _():
        carry_ref[...] = jnp.zeros_like(carry_ref)

    h = _modulated_norm(x_ref[0], ng_ref, sc_ref, sh_ref)

    def proj(w_ref, lo=None, hi=None):
        w = w_ref[...] if lo is None else w_ref[:, lo:hi]
        return jnp.dot(h, w, preferred_element_type=F32)

    lane = lax.broadcasted_iota(jnp.int32, (tm, LANES), 1)
    pos_col = jnp.broadcast_to(pos_ref[0], (8, tm)).T[:, 0:1]
    cos, s_lo, s_hi = _rope_tables(pos_col, freq_ref[...])

    lat = proj(wlat_ref)
    lat_q = lat[:, 0:MLA_Q_RANK]
    lat_kv = lat[:, MLA_Q_RANK:MLA_Q_RANK + MLA_KV_RANK]
    misc = lat[:, MLA_Q_RANK + MLA_KV_RANK:]
    g_ref[0, :, 0:D_MODEL] = _sigmoid(proj(wgate_ref, 0, D_MODEL)).astype(BF16)

    cqn = (_rms_scale(lat_q) * qg_ref[...]).astype(BF16)
    ckvn = (_rms_scale(lat_kv) * kvg_ref[...]).astype(BF16)
    qm = jnp.dot(cqn, wuq_ref[...], preferred_element_type=F32)
    kvm = jnp.dot(ckvn, wukv_ref[...], preferred_element_type=F32)
    kn, vm = kvm[:, 0:MLA_HEADS * MLA_NOPE], kvm[:, MLA_HEADS * MLA_NOPE:]
    fv = proj(wfox_ref, 2 * FOX_HEADS * FOX_DIM, 3 * FOX_HEADS * FOX_DIM)
    for hd in range(MLA_HEADS):
        blk = qm[:, hd * HEAD_PAD:(hd + 1) * HEAD_PAD]
        qt_ref[0, hd * HEAD_PAD:(hd + 1) * HEAD_PAD, :] = (
            _rope_block(blk, cos, s_lo, s_hi) * (MLA_SCALE * LOG2E)).T.astype(BF16)

    krope = _rope_block(jnp.where(lane < FLOGIT_LO, misc, 0.0), cos, s_lo, s_hi)
    for hd in range(MLA_HEADS):
        kb = kn[:, (hd // 2) * LANES:(hd // 2 + 1) * LANES]
        if hd % 2:
            kb = pltpu.roll(kb, MLA_NOPE, 1)
        k_ref[0, :, hd * HEAD_PAD:(hd + 1) * HEAD_PAD] = jnp.where(
            lane < MLA_NOPE, kb, krope).astype(BF16)
    vt_ref[0, 0:MLA_HEADS * MLA_V, :] = vm.T.astype(BF16)
    vt_ref[0, MLA_HEADS * MLA_V:, :] = fv.T.astype(BF16)

    g_ref[0, :, D_MODEL:] = _sigmoid(proj(wgate_ref, D_MODEL, 2 * D_MODEL)).astype(BF16)

    z = misc + bf_ref[...]
    logf = jnp.minimum(z, 0.0) - jnp.log1p(jnp.exp(-jnp.abs(z)))
    logf = jnp.where((lane >= FLOGIT_LO) & (lane < FLOGIT_LO + FOX_HEADS), logf, 0.0)
    row = lax.broadcasted_iota(jnp.int32, (tm, tm), 0)
    col = lax.broadcasted_iota(jnp.int32, (tm, tm), 1)
    tri = jnp.where(col <= row, 1.0, 0.0).astype(BF16)
    parts = jnp.dot(tri, _pack3(logf), preferred_element_type=F32)
    total = (parts + pltpu.roll(parts, LANES - FOX_HEADS, 1)
             + pltpu.roll(parts, LANES - 2 * FOX_HEADS, 1))
    fcum = carry_ref[...] + jnp.where(
        (lane >= FLOGIT_LO) & (lane < FLOGIT_LO + FOX_HEADS), total, 0.0)
    carry_ref[...] = fcum[tm - 1:tm, :]
    fq = proj(wfox_ref, 0, FOX_HEADS * FOX_DIM) * (FOX_SCALE * LOG2E)
    fk = proj(wfox_ref, FOX_HEADS * FOX_DIM, 2 * FOX_HEADS * FOX_DIM)

    bias3 = _pack3(fcum * LOG2E)
    aug_q = jnp.dot(bias3, sq_ref[...], preferred_element_type=F32) + qc_ref[...]
    aug_k = jnp.dot(bias3, sk_ref[...], preferred_element_type=F32) + kc_ref[...]
    for hd in range(FOX_HEADS):
        pair = slice((hd // 2) * LANES, (hd // 2 + 1) * LANES)
        blk = slice(hd * HEAD_PAD, (hd + 1) * HEAD_PAD)
        qb, kb = fq[:, pair], fk[:, pair]
        if hd % 2:
            qb, kb = pltpu.roll(qb, FOX_DIM, 1), pltpu.roll(kb, FOX_DIM, 1)
        c0 = (MLA_HEADS + hd) * HEAD_PAD
        qt_ref[0, c0:c0 + HEAD_PAD, :] = jnp.where(lane < FOX_DIM, qb, aug_q[:, blk]).T.astype(BF16)
        k_ref[0, :, c0:c0 + HEAD_PAD] = jnp.where(lane < FOX_DIM, kb, aug_k[:, blk]).astype(BF16)


def _resident(shape):
    return pl.BlockSpec(shape, lambda bi, si: (0,) * len(shape), pipeline_mode=pl.Buffered(1))


def _inproj_call(x, pos_f, norm_g, sc_mix, sh_mix, freq_row, bf_row, qg, kvg, weights, tm):
    b, s, d = x.shape
    tok = lambda w: pl.BlockSpec((1, tm, w), lambda bi, si: (bi, si, 0))
    vec = pl.BlockSpec((1, 1, d), lambda bi, si: (bi, 0, 0))
    return pl.pallas_call(
        functools.partial(_inproj_kernel, tm=tm),
        grid=(b, s // tm),
        in_specs=[tok(d), pl.BlockSpec((1, 1, tm), lambda bi, si: (bi, 0, si)),
                  _resident((1, d)), vec, vec, _resident((1, LANES)),
                  _resident((1, LANES)), _resident((1, MLA_Q_RANK)), _resident((1, MLA_KV_RANK))]
                 + [_resident(w.shape) for w in weights],
        out_specs=[pl.BlockSpec((1, N_HEADS * HEAD_PAD, tm), lambda bi, si: (bi, 0, si)),
                   tok(N_HEADS * HEAD_PAD),
                   pl.BlockSpec((1, N_HEADS * FOX_DIM, tm), lambda bi, si: (bi, 0, si)),
                   tok(2 * d)],
        out_shape=[jax.ShapeDtypeStruct((b, N_HEADS * HEAD_PAD, s), BF16),
                   jax.ShapeDtypeStruct((b, s, N_HEADS * HEAD_PAD), BF16),
                   jax.ShapeDtypeStruct((b, N_HEADS * FOX_DIM, s), BF16),
                   jax.ShapeDtypeStruct((b, s, 2 * d), BF16)],
        scratch_shapes=[pltpu.VMEM((1, LANES), F32)],
        compiler_params=pltpu.CompilerParams(dimension_semantics=("arbitrary", "arbitrary"),
                                             vmem_limit_bytes=VMEM_LIMIT),
        name="inproj",
    )(x, pos_f, norm_g, sc_mix, sh_mix, freq_row, bf_row, qg, kvg, *weights)


def _attn_kernel(qt_ref, k_ref, vt_ref, o_ref, st_ref, mx_ref, m_ref, acc_ref, *, tq, tk, nsub):
    qi = pl.program_id(2)
    assert tq == tk
    row = lax.broadcasted_iota(jnp.int32, (tk, tq), 0)
    col = lax.broadcasted_iota(jnp.int32, (tk, tq), 1)
    causal = row <= col
    ones_rows = jnp.ones((ONES_ROWS, tk), BF16)
    streams = [(hh, sub) for sub in range(nsub) for hh in range(2)]

    def scores(idx, j, slot, full):
        hh, sub = streams[idx]
        start = pl.multiple_of(j * tk, tk)
        qt = qt_ref[0, hh * HEAD_PAD:(hh + 1) * HEAD_PAD, sub * tq:(sub + 1) * tq]
        k = k_ref[0, pl.ds(start, tk), hh * HEAD_PAD:(hh + 1) * HEAD_PAD]
        st = jnp.dot(k, qt, preferred_element_type=F32)
        if full is False:
            st = jnp.where(causal, st, -jnp.inf)
        elif full is not True:
            st = jnp.where(row <= col + jnp.where(full, tk, 0), st, -jnp.inf)
        st_ref[slot, idx] = st
        mx_ref[slot, idx] = jnp.max(st, axis=0, keepdims=True)

    def update(idx, j, slot):
        hh, _ = streams[idx]
        start = pl.multiple_of(j * tk, tk)
        m = m_ref[idx]
        m_new = jnp.maximum(m, mx_ref[slot, idx])
        m_ref[idx] = m_new
        p = jnp.exp2(st_ref[slot, idx] - m_new).astype(BF16)
        vt = vt_ref[0, hh * MLA_V:(hh + 1) * MLA_V, pl.ds(start, tk)]
        vt_aug = jnp.concatenate([vt, ones_rows], axis=0)
        acc_ref[idx] = (jnp.exp2(m - m_new) * acc_ref[idx]
                        + jnp.dot(vt_aug, p, preferred_element_type=F32))

    def step(j, slot, diag_sub=None, next_is_full=None):
        first = 0 if diag_sub is None else diag_sub
        for idx, (_, sub) in enumerate(streams):
            if diag_sub is None:
                scores(idx, j + 1, 1 - slot, full=next_is_full if sub == 0 else True)
            elif sub > first:
                scores(idx, j + 1, 1 - slot, full=(sub != diag_sub + 1))
            if sub >= first:
                update(idx, j, slot)

    m_ref[...] = jnp.full(m_ref.shape, -jnp.inf, F32)
    acc_ref[...] = jnp.zeros(acc_ref.shape, F32)
    assert nsub % 2 == 0
    n_full = qi * nsub
    for idx, (_, sub) in enumerate(streams):
        scores(idx, 0, 0, full=(qi > 0) if sub == 0 else True)

    def body(i, carry):
        for u in range(nsub):
            step(nsub * i + u, u % 2, next_is_full=True if u < nsub - 1 else i < qi - 1)
        return carry

    lax.fori_loop(0, qi, body, 0)
    for r in range(nsub):
        step(n_full + r, r % 2, diag_sub=r)
        outs = []
        for hh in range(2):
            acc = acc_ref[streams.index((hh, r))]
            outs.append(acc[0:MLA_V, :] / acc[MLA_V:MLA_V + 1, :])
        o_ref[0, r * tq:(r + 1) * tq, :] = jnp.concatenate(outs, axis=0).T.astype(BF16)


def _attn_call(qt_all, k_all, vt_all, tq, nsub):
    b, s, _ = k_all.shape
    pairs = N_HEADS // 2
    bq = tq * nsub
    return pl.pallas_call(
        functools.partial(_attn_kernel, tq=tq, tk=tq, nsub=nsub),
        grid=(b, pairs, s // bq),
        in_specs=[pl.BlockSpec((1, 2 * HEAD_PAD, bq), lambda bi, pi, qi: (bi, pi, qi)),
                  pl.BlockSpec((1, s, 2 * HEAD_PAD), lambda bi, pi, qi: (bi, 0, pi)),
                  pl.BlockSpec((1, LANES, s), lambda bi, pi, qi: (bi, pi, 0))],
        out_specs=pl.BlockSpec((1, bq, LANES), lambda bi, pi, qi: (bi, qi, pi)),
        out_shape=jax.ShapeDtypeStruct((b, s, pairs * LANES), BF16),
        scratch_shapes=[pltpu.VMEM((2, 2 * nsub, tq, tq), F32),
                        pltpu.VMEM((2, 2 * nsub, 1, tq), F32),
                        pltpu.VMEM((2 * nsub, 1, tq), F32),
                        pltpu.VMEM((2 * nsub, MLA_V + ONES_ROWS, tq), F32)],
        compiler_params=pltpu.CompilerParams(
            dimension_semantics=("arbitrary", "arbitrary", "arbitrary"),
            vmem_limit_bytes=VMEM_LIMIT),
        name="attn",
    )(qt_all, k_all, vt_all)


def _mixer_residual(o_ref, g_ref, x_ref, gm_ref, wm_ref, wf_ref, wo_ref):
    half = MLA_HEADS * MLA_V
    o = o_ref[0]
    g = g_ref[0].astype(F32)
    y = (g[:, :D_MODEL] * jnp.dot(o[:, :half], wm_ref[...], preferred_element_type=F32)
         + g[:, D_MODEL:] * jnp.dot(o[:, half:], wf_ref[...], preferred_element_type=F32))
    mix = jnp.dot(y.astype(BF16), wo_ref[...], preferred_element_type=F32)
    return x_ref[0] + gm_ref[0] * mix


def _shift_rows(u, prev, k):
    tm = u.shape[0]
    body = pltpu.roll(u, k, 0)
    head = jnp.where(lax.broadcasted_iota(jnp.int32, prev.shape, 0) < k,
                     pltpu.roll(prev, k, 0), body[0:8, :])
    return jnp.concatenate([head, body[8:tm, :]], axis=0)


def _tail_kernel(o_ref, g_ref, x_ref, gm_ref, wm_ref, wf_ref, wo_ref,
                 ng_ref, sc_ref, sh_ref, gf_ref, cw_ref, cb_ref, fg_ref, wup_ref, wdn_ref,
                 out_ref, tail_ref, act_ref, *, chunk):
    s_idx = pl.program_id(1)

    @pl.when(s_idx == 0)
    def _():
        tail_ref[...] = jnp.zeros_like(tail_ref)

    x1 = _mixer_residual(o_ref, g_ref, x_ref, gm_ref, wm_ref, wf_ref, wo_ref)
    tm = x1.shape[0]
    h = _modulated_norm(x1, ng_ref, sc_ref, sh_ref)

    def conv(c0):
        u = jnp.dot(h, wup_ref[:, c0:c0 + chunk], preferred_element_type=F32)
        prev = tail_ref[:, c0:c0 + chunk]
        tail_ref[:, c0:c0 + chunk] = u[tm - 8:tm, :]
        w = cw_ref[:, c0:c0 + chunk]
        return (cb_ref[:, c0:c0 + chunk] + w[0:1] * _shift_rows(u, prev, 2)
                + w[1:2] * _shift_rows(u, prev, 1) + w[2:3] * u)

    for c0 in range(0, D_FF, chunk):
        act_ref[:, c0:c0 + chunk] = (_silu(conv(c0)) * conv(D_FF + c0)).astype(BF16)

    y = jnp.dot(act_ref[...], wdn_ref[...], preferred_element_type=F32)
    x2 = x1 + gf_ref[0] * y
    out_ref[0] = _rms_scale(x2) * fg_ref[...]


def _tail_call(o_all, g_all, x, g_m, wm, wf, wo, norm_g, sc_ffn, sh_ffn, g_f, conv_w, conv_b,
               final_g, wup, wdn, tm, chunk):
    b, s, d = x.shape
    const = _resident
    tok = lambda w: pl.BlockSpec((1, tm, w), lambda bi, si: (bi, si, 0))
    vec = pl.BlockSpec((1, 1, d), lambda bi, si: (bi, 0, 0))
    return pl.pallas_call(
        functools.partial(_tail_kernel, chunk=chunk),
        grid=(b, s // tm),
        in_specs=[tok(o_all.shape[-1]), tok(g_all.shape[-1]), tok(d), vec,
                  const(wm.shape), const(wf.shape), const(wo.shape),
                  const((1, d)), vec, vec, vec, const(conv_w.shape), const(conv_b.shape),
                  const(final_g.shape), const(wup.shape), const(wdn.shape)],
        out_specs=tok(d),
        out_shape=jax.ShapeDtypeStruct((b, s, d), F32),
        scratch_shapes=[pltpu.VMEM((8, 2 * D_FF), F32), pltpu.VMEM((tm, D_FF), BF16)],
        compiler_params=pltpu.CompilerParams(dimension_semantics=("arbitrary", "arbitrary"),
                                             vmem_limit_bytes=VMEM_LIMIT),
        name="tail",
    )(o_all, g_all, x, g_m, wm, wf, wo, norm_g, sc_ffn, sh_ffn, g_f, conv_w, conv_b, final_g,
      wup, wdn)


def _pad_heads(w, heads, width):
    k = w.shape[0]
    w = w.reshape(k, heads, width)
    return jnp.pad(w, ((0, 0), (0, 0), (0, HEAD_PAD - width))).reshape(k, heads * HEAD_PAD)


def _prep_w_in(w_in):
    d = w_in.shape[0]
    splits = (MLA_Q_RANK, MLA_KV_RANK, MLA_ROPE, FOX_HEADS * FOX_DIM, FOX_HEADS * FOX_DIM,
              FOX_HEADS * FOX_DIM, FOX_HEADS, D_MODEL, D_MODEL)
    offs = [0]
    for n in splits:
        offs.append(offs[-1] + n)
    w_in = w_in.astype(BF16)
    piece = lambda i, j=None: w_in[:, offs[i]:offs[i + 1 if j is None else j]]
    misc = jnp.concatenate([jnp.zeros((d, MLA_NOPE), BF16), piece(2), piece(6),
                            jnp.zeros((d, LANES - FLOGIT_LO - FOX_HEADS), BF16)], axis=1)
    return [jnp.concatenate([piece(0, 2), misc], axis=1), piece(3, 6), piece(7, 9)]


def _bias_placement():
    part, head = np.meshgrid(np.arange(3), np.arange(FOX_HEADS), indexing="ij")
    rows = (FLOGIT_LO + part * FOX_HEADS + head).reshape(-1)
    q_cols = (head * HEAD_PAD + AUG_LO + part).reshape(-1)
    k_cols = (head * HEAD_PAD + AUG_LO + 3 + part).reshape(-1)
    width = FOX_HEADS * HEAD_PAD
    sq, sk = np.zeros((LANES, width), np.float32), np.zeros((LANES, width), np.float32)
    qc, kc = np.zeros((1, width), np.float32), np.zeros((1, width), np.float32)
    sq[rows, q_cols] = 1.0
    sk[rows, k_cols] = -1.0
    qc[0, k_cols] = 1.0
    kc[0, q_cols] = 1.0
    return jnp.asarray(sq, BF16), jnp.asarray(sk, BF16), jnp.asarray(qc), jnp.asarray(kc)


def kernel(x, c, positions, w_ada, b_ada, norm_mix_g, w_in, q_norm_g, w_uq, kv_norm_g, w_ukv,
           b_forget, w_o_mla, w_o_fox, w_out, norm_ffn_g, w_up, conv_w, conv_b, w_down,
           norm_final_g):
    assert w_ada.shape[0] == 1, "single-layer block"
    b, s, d = x.shape
    ada = _ada_call(c, w_ada[0], b_ada[0])
    sh_m, sc_m, g_m, sh_f, sc_f, g_f = (ada[:, i * d:(i + 1) * d].reshape(b, 1, d)
                                        for i in range(N_ADA))

    inv_freq = ROPE_THETA ** (-jnp.arange(0, MLA_ROPE, 2, dtype=F32) / MLA_ROPE)
    freq_row = jnp.concatenate([jnp.zeros((MLA_NOPE,), F32), inv_freq, inv_freq,
                                jnp.zeros((LANES - FLOGIT_LO,), F32)]).reshape(1, LANES)
    bf_row = jnp.concatenate([jnp.zeros((FLOGIT_LO,), F32), b_forget[0].astype(F32),
                              jnp.zeros((LANES - FLOGIT_LO - FOX_HEADS,), F32)]).reshape(1, LANES)
    pos_f = positions.astype(F32).reshape(b, 1, s)

    wuq = _pad_heads(w_uq[0], MLA_HEADS, MLA_NOPE + MLA_ROPE).astype(BF16)
    wukv = w_ukv[0].reshape(MLA_KV_RANK, MLA_HEADS, MLA_NOPE + MLA_V)
    wukv = jnp.concatenate([wukv[:, :, :MLA_NOPE].reshape(MLA_KV_RANK, -1),
                            wukv[:, :, MLA_NOPE:].reshape(MLA_KV_RANK, -1)], axis=1).astype(BF16)

    qt_all, k_all, vt_all, g_all = _inproj_call(
        x, pos_f, norm_mix_g[0].reshape(1, d), sc_m, sh_m, freq_row, bf_row,
        q_norm_g[0].reshape(1, -1), kv_norm_g[0].reshape(1, -1),
        _prep_w_in(w_in[0]) + [wuq, wukv] + list(_bias_placement()), tm=512)
    o_all = _attn_call(qt_all, k_all, vt_all, tq=256, nsub=8)
    return _tail_call(o_all, g_all, x, g_m, w_o_mla[0].astype(BF16), w_o_fox[0].astype(BF16),
                      w_out[0].astype(BF16), norm_ffn_g[0].reshape(1, d), sc_f, sh_f, g_f, conv_w[0],
                      conv_b[0].reshape(1, -1), norm_final_g.reshape(1, -1),
                      w_up[0].astype(BF16), w_down[0].astype(BF16), tm=512, chunk=256)
```

```python
import functools
import math

import jax
import jax.numpy as jnp
import numpy as np
from jax import lax
from jax.experimental import pallas as pl
from jax.experimental.pallas import tpu as pltpu

F32 = jnp.float32
BF16 = jnp.bfloat16

D_MODEL = 1024
MLA_HEADS = 8
MLA_Q_RANK = 384
MLA_KV_RANK = 256
MLA_NOPE = 64
MLA_ROPE = 32
MLA_V = 64
MLA_SCALE = 1.0 / math.sqrt(MLA_NOPE + MLA_ROPE)
ROPE_THETA = 10000.0
FOX_HEADS = 8
FOX_DIM = 64
FOX_SCALE = 1.0 / math.sqrt(FOX_DIM)
LOG2E = math.log2(math.e)
D_FF = 2816
CONV_WIDTH = 3
EPS = 1e-6
N_ADA = 6

LANES = 128
HEAD_PAD = LANES
N_HEADS = MLA_HEADS + FOX_HEADS
ROPE_HALF = MLA_ROPE // 2
ROPE_LO = MLA_NOPE
ROPE_HI = MLA_NOPE + ROPE_HALF
FLOGIT_LO = MLA_NOPE + MLA_ROPE
AUG_LO = FOX_DIM
ONES_ROWS = 16
PERM_PITCH = 72

VMEM_LIMIT = 56 * 1024 * 1024


def _silu(x):
    return x * (1.0 / (1.0 + jnp.exp(-x)))


def _sigmoid(x):
    return 1.0 / (1.0 + jnp.exp(-x))


def _rms_scale(x):
    return x * lax.rsqrt(jnp.mean(x * x, axis=-1, keepdims=True) + EPS)


def _ada_kernel(ct_ref, w_ref, b_ref, o_ref, *, batch):
    act = _silu(ct_ref[...])
    w = w_ref[...]
    for b in range(batch):
        o_ref[b:b + 1, :] = jnp.sum(w * act[:, b:b + 1], axis=0, keepdims=True) + b_ref[...]


def _ada_call(c, w_ada, b_ada):
    batch, d = c.shape
    n = w_ada.shape[1]
    tn = 1536
    return pl.pallas_call(
        functools.partial(_ada_kernel, batch=batch),
        grid=(n // tn,),
        in_specs=[pl.BlockSpec((d, batch), lambda j: (0, 0)),
                  pl.BlockSpec((d, tn), lambda j: (0, j)),
                  pl.BlockSpec((1, tn), lambda j: (0, j))],
        out_specs=pl.BlockSpec((batch, tn), lambda j: (0, j)),
        out_shape=jax.ShapeDtypeStruct((batch, n), F32),
        compiler_params=pltpu.CompilerParams(dimension_semantics=("arbitrary",),
                                             vmem_limit_bytes=VMEM_LIMIT),
        name="ada",
    )(c.T, w_ada, b_ada.reshape(1, n))


def _rope_tables(pos_col, freq_row):
    ang = pos_col * freq_row
    lane = lax.broadcasted_iota(jnp.int32, ang.shape, 1)
    cos = jnp.cos(ang)
    sin = jnp.sin(ang)
    s_lo = jnp.where((lane >= ROPE_LO) & (lane < ROPE_HI), -sin, 0.0)
    s_hi = jnp.where((lane >= ROPE_HI) & (lane < FLOGIT_LO), sin, 0.0)
    return cos, s_lo, s_hi


def _rope_block(t, cos, s_lo, s_hi):
    return (t * cos + pltpu.roll(t, LANES - ROPE_HALF, 1) * s_lo
            + pltpu.roll(t, ROPE_HALF, 1) * s_hi)


def _split3(x):
    hi = x.astype(BF16)
    r1 = x - hi.astype(F32)
    mid = r1.astype(BF16)
    lo = (r1 - mid.astype(F32)).astype(BF16)
    return hi, mid, lo


def _pack3(x):
    hi, mid, lo = (p.astype(F32) for p in _split3(x))
    return (hi + pltpu.roll(mid, FOX_HEADS, 1) + pltpu.roll(lo, 2 * FOX_HEADS, 1)).astype(BF16)


def _modulated_norm(x, gain_ref, sc_ref, sh_ref):
    return (_rms_scale(x) * (gain_ref[...] * (1.0 + sc_ref[0])) + sh_ref[0]).astype(BF16)


def _inproj_kernel(x_ref, pos_ref, ng_ref, sc_ref, sh_ref, freq_ref, bf_ref, qg_ref, kvg_ref,
                   wlat_ref, wfox_ref, wgate_ref,
                   wuq_ref, wukv_ref, sq_ref, sk_ref, qc_ref, kc_ref,
                   qt_ref, k_ref, vt_ref, g_ref, carry_ref, *, tm):
    s_idx = pl.program_id(1)

    @pl.when(s_idx == 0)
    def _():
        carry_ref[...] = jnp.zeros_like(carry_ref)

    h = _modulated_norm(x_ref[0], ng_ref, sc_ref, sh_ref)

    def proj(w_ref, lo=None, hi=None):
        w = w_ref[...] if lo is None else w_ref[:, lo:hi]
        return jnp.dot(h, w, preferred_element_type=F32)

    lane = lax.broadcasted_iota(jnp.int32, (tm, LANES), 1)
    pos_col = jnp.broadcast_to(pos_ref[0], (8, tm)).T[:, 0:1]
    cos, s_lo, s_hi = _rope_tables(pos_col, freq_ref[...])

    lat = proj(wlat_ref)
    lat_q = lat[:, 0:MLA_Q_RANK]
    lat_kv = lat[:, MLA_Q_RANK:MLA_Q_RANK + MLA_KV_RANK]
    misc = lat[:, MLA_Q_RANK + MLA_KV_RANK:]
    g_ref[0, :, 0:D_MODEL] = _sigmoid(proj(wgate_ref, 0, D_MODEL)).astype(BF16)

    cqn = (_rms_scale(lat_q) * qg_ref[...]).astype(BF16)
    ckvn = (_rms_scale(lat_kv) * kvg_ref[...]).astype(BF16)
    qm = jnp.dot(cqn, wuq_ref[...], preferred_element_type=F32)
    kvm = jnp.dot(ckvn, wukv_ref[...], preferred_element_type=F32)
    kn, vm = kvm[:, 0:MLA_HEADS * MLA_NOPE], kvm[:, MLA_HEADS * MLA_NOPE:]
    fv = proj(wfox_ref, 2 * FOX_HEADS * FOX_DIM, 3 * FOX_HEADS * FOX_DIM)
    for hd in range(MLA_HEADS):
        blk = qm[:, hd * HEAD_PAD:(hd + 1) * HEAD_PAD]
        qt_ref[0, hd * HEAD_PAD:(hd + 1) * HEAD_PAD, :] = (
            _rope_block(blk, cos, s_lo, s_hi) * (MLA_SCALE * LOG2E)).T.astype(BF16)

    krope = _rope_block(jnp.where(lane < FLOGIT_LO, misc, 0.0), cos, s_lo, s_hi)
    for hd in range(MLA_HEADS):
        kb = kn[:, (hd // 2) * LANES:(hd // 2 + 1) * LANES]
        if hd % 2:
            kb = pltpu.roll(kb, MLA_NOPE, 1)
        k_ref[0, :, hd * HEAD_PAD:(hd + 1) * HEAD_PAD] = jnp.where(
            lane < MLA_NOPE, kb, krope).astype(BF16)
    vt_ref[0, 0:MLA_HEADS * MLA_V, :] = vm.T.astype(BF16)
    vt_ref[0, MLA_HEADS * MLA_V:, :] = fv.T.astype(BF16)

    g_ref[0, :, D_MODEL:] = _sigmoid(proj(wgate_ref, D_MODEL, 2 * D_MODEL)).astype(BF16)

    z = misc + bf_ref[...]
    logf = jnp.minimum(z, 0.0) - jnp.log1p(jnp.exp(-jnp.abs(z)))
    logf = jnp.where((lane >= FLOGIT_LO) & (lane < FLOGIT_LO + FOX_HEADS), logf, 0.0)
    row = lax.broadcasted_iota(jnp.int32, (tm, tm), 0)
    col = lax.broadcasted_iota(jnp.int32, (tm, tm), 1)
    tri = jnp.where(col <= row, 1.0, 0.0).astype(BF16)
    parts = jnp.dot(tri, _pack3(logf), preferred_element_type=F32)
    total = (parts + pltpu.roll(parts, LANES - FOX_HEADS, 1)
             + pltpu.roll(parts, LANES - 2 * FOX_HEADS, 1))
    fcum = carry_ref[...] + jnp.where(
        (lane >= FLOGIT_LO) & (lane < FLOGIT_LO + FOX_HEADS), total, 0.0)
    carry_ref[...] = fcum[tm - 1:tm, :]
    fq = proj(wfox_ref, 0, FOX_HEADS * FOX_DIM) * (FOX_SCALE * LOG2E)
    fk = proj(wfox_ref, FOX_HEADS * FOX_DIM, 2 * FOX_HEADS * FOX_DIM)

    bias3 = _pack3(fcum * LOG2E)
    aug_q = jnp.dot(bias3, sq_ref[...], preferred_element_type=F32) + qc_ref[...]
    aug_k = jnp.dot(bias3, sk_ref[...], preferred_element_type=F32) + kc_ref[...]
    for hd in range(FOX_HEADS):
        pair = slice((hd // 2) * LANES, (hd // 2 + 1) * LANES)
        blk = slice(hd * HEAD_PAD, (hd + 1) * HEAD_PAD)
        qb, kb = fq[:, pair], fk[:, pair]
        if hd % 2:
            qb, kb = pltpu.roll(qb, FOX_DIM, 1), pltpu.roll(kb, FOX_DIM, 1)
        c0 = (MLA_HEADS + hd) * HEAD_PAD
        qt_ref[0, c0:c0 + HEAD_PAD, :] = jnp.where(lane < FOX_DIM, qb, aug_q[:, blk]).T.astype(BF16)
        k_ref[0, :, c0:c0 + HEAD_PAD] = jnp.where(lane < FOX_DIM, kb, aug_k[:, blk]).astype(BF16)


def _resident(shape):
    return pl.BlockSpec(shape, lambda bi, si: (0,) * len(shape), pipeline_mode=pl.Buffered(1))


def _inproj_call(x, pos_f, norm_g, sc_mix, sh_mix, freq_row, bf_row, qg, kvg, weights, tm):
    b, s, d = x.shape
    tok = lambda w: pl.BlockSpec((1, tm, w), lambda bi, si: (bi, si, 0))
    vec = pl.BlockSpec((1, 1, d), lambda bi, si: (bi, 0, 0))
    return pl.pallas_call(
        functools.partial(_inproj_kernel, tm=tm),
        grid=(b, s // tm),
        in_specs=[tok(d), pl.BlockSpec((1, 1, tm), lambda bi, si: (bi, 0, si)),
                  _resident((1, d)), vec, vec, _resident((1, LANES)),
                  _resident((1, LANES)), _resident((1, MLA_Q_RANK)), _resident((1, MLA_KV_RANK))]
                 + [_resident(w.shape) for w in weights],
        out_specs=[pl.BlockSpec((1, N_HEADS * HEAD_PAD, tm), lambda bi, si: (bi, 0, si)),
                   tok(N_HEADS * HEAD_PAD),
                   pl.BlockSpec((1, N_HEADS * FOX_DIM, tm), lambda bi, si: (bi, 0, si)),
                   tok(2 * d)],
        out_shape=[jax.ShapeDtypeStruct((b, N_HEADS * HEAD_PAD, s), BF16),
                   jax.ShapeDtypeStruct((b, s, N_HEADS * HEAD_PAD), BF16),
                   jax.ShapeDtypeStruct((b, N_HEADS * FOX_DIM, s), BF16),
                   jax.ShapeDtypeStruct((b, s, 2 * d), BF16)],
        scratch_shapes=[pltpu.VMEM((1, LANES), F32)],
        compiler_params=pltpu.CompilerParams(dimension_semantics=("arbitrary", "arbitrary"),
                                             vmem_limit_bytes=VMEM_LIMIT),
        name="inproj",
    )(x, pos_f, norm_g, sc_mix, sh_mix, freq_row, bf_row, qg, kvg, *weights)


def _attn_kernel(qt_ref, k_ref, vt_ref, o_ref, st_ref, mx_ref, m_ref, acc_ref, *, tq, tk, nsub):
    qi = pl.program_id(2)
    assert tq == tk
    row = lax.broadcasted_iota(jnp.int32, (tk, tq), 0)
    col = lax.broadcasted_iota(jnp.int32, (tk, tq), 1)
    causal = row <= col
    ones_rows = jnp.ones((ONES_ROWS, tk), BF16)
    streams = [(hh, sub) for sub in range(nsub) for hh in range(2)]

    def scores(idx, j, slot, full):
        hh, sub = streams[idx]
        start = pl.multiple_of(j * tk, tk)
        qt = qt_ref[0, hh * HEAD_PAD:(hh + 1) * HEAD_PAD, sub * tq:(sub + 1) * tq]
        k = k_ref[0, pl.ds(start, tk), hh * HEAD_PAD:(hh + 1) * HEAD_PAD]
        st = jnp.dot(k, qt, preferred_element_type=F32)
        if full is False:
            st = jnp.where(causal, st, -jnp.inf)
        elif full is not True:
            st = jnp.where(row <= col + jnp.where(full, tk, 0), st, -jnp.inf)
        st_ref[slot, idx] = st
        mx_ref[slot, idx] = jnp.max(st, axis=0, keepdims=True)

    def update(idx, j, slot):
        hh, _ = streams[idx]
        start = pl.multiple_of(j * tk, tk)
        m = m_ref[idx]
        m_new = jnp.maximum(m, mx_ref[slot, idx])
        m_ref[idx] = m_new
        p = jnp.exp2(st_ref[slot, idx] - m_new).astype(BF16)
        vt = vt_ref[0, hh * MLA_V:(hh + 1) * MLA_V, pl.ds(start, tk)]
        vt_aug = jnp.concatenate([vt, ones_rows], axis=0)
        acc_ref[idx] = (jnp.exp2(m - m_new) * acc_ref[idx]
                        + jnp.dot(vt_aug, p, preferred_element_type=F32))

    def step(j, slot, diag_sub=None, next_is_full=None):
        first = 0 if diag_sub is None else diag_sub
        for idx, (_, sub) in enumerate(streams):
            if diag_sub is None:
                scores(idx, j + 1, 1 - slot, full=next_is_full if sub == 0 else True)
            elif sub > first:
                scores(idx, j + 1, 1 - slot, full=(sub != diag_sub + 1))
            if sub >= first:
                update(idx, j, slot)

    m_ref[...] = jnp.full(m_ref.shape, -jnp.inf, F32)
    acc_ref[...] = jnp.zeros(acc_ref.shape, F32)
    assert nsub % 2 == 0
    n_full = qi * nsub
    for idx, (_, sub) in enumerate(streams):
        scores(idx, 0, 0, full=(qi > 0) if sub == 0 else True)

    def body(i, carry):
        for u in range(nsub):
            step(nsub * i + u, u % 2, next_is_full=True if u < nsub - 1 else i < qi - 1)
        return carry

    lax.fori_loop(0, qi, body, 0)
    for r in range(nsub):
        step(n_full + r, r % 2, diag_sub=r)
        outs = []
        for hh in range(2):
            acc = acc_ref[streams.index((hh, r))]
            outs.append(acc[0:MLA_V, :] / acc[MLA_V:MLA_V + 1, :])
        o_ref[0, r * tq:(r + 1) * tq, :] = jnp.concatenate(outs, axis=0).T.astype(BF16)


def _attn_call(qt_all, k_all, vt_all, tq, nsub):
    b, s, _ = k_all.shape
    pairs = N_HEADS // 2
    bq = tq * nsub
    return pl.pallas_call(
        functools.partial(_attn_kernel, tq=tq, tk=tq, nsub=nsub),
        grid=(b, pairs, s // bq),
        in_specs=[pl.BlockSpec((1, 2 * HEAD_PAD, bq), lambda bi, pi, qi: (bi, pi, qi)),
                  pl.BlockSpec((1, s, 2 * HEAD_PAD), lambda bi, pi, qi: (bi, 0, pi)),
                  pl.BlockSpec((1, LANES, s), lambda bi, pi, qi: (bi, pi, 0))],
        out_specs=pl.BlockSpec((1, bq, LANES), lambda bi, pi, qi: (bi, qi, pi)),
        out_shape=jax.ShapeDtypeStruct((b, s, pairs * LANES), BF16),
        scratch_shapes=[pltpu.VMEM((2, 2 * nsub, tq, tq), F32),
                        pltpu.VMEM((2, 2 * nsub, 1, tq), F32),
                        pltpu.VMEM((2 * nsub, 1, tq), F32),
                        pltpu.VMEM((2 * nsub, MLA_V + ONES_ROWS, tq), F32)],
        compiler_params=pltpu.CompilerParams(
            dimension_semantics=("arbitrary", "arbitrary", "arbitrary"),
            vmem_limit_bytes=VMEM_LIMIT),
        name="attn",
    )(qt_all, k_all, vt_all)


def _mixer_residual(o_ref, g_ref, x_ref, gm_ref, wm_ref, wf_ref, wo_ref):
    half = MLA_HEADS * MLA_V
    o = o_ref[0]
    g = g_ref[0].astype(F32)
    y = (g[:, :D_MODEL] * jnp.dot(o[:, :half], wm_ref[...], preferred_element_type=F32)
         + g[:, D_MODEL:] * jnp.dot(o[:, half:], wf_ref[...], preferred_element_type=F32))
    mix = jnp.dot(y.astype(BF16), wo_ref[...], preferred_element_type=F32)
    return x_ref[0] + gm_ref[0] * mix


def _interleave_rows(x, perm_ref):
    tm, d = x.shape
    group = tm // 8
    for ls in range(d // LANES):
        for r in range(8):
            perm_ref[ls, r * PERM_PITCH:r * PERM_PITCH + group, :] = (
                x[r * group:(r + 1) * group, ls * LANES:(ls + 1) * LANES])
    return jnp.concatenate(
        [jnp.concatenate([perm_ref[ls, pl.ds(i, 8, stride=PERM_PITCH), :] for i in range(group)],
                         axis=0) for ls in range(d // LANES)], axis=1)


def _store_natural_rows(xp, perm_ref, out_ref):
    tm, d = xp.shape
    group = tm // 8
    for ls in range(d // LANES):
        for i in range(group):
            perm_ref[ls, pl.ds(i, 8, stride=PERM_PITCH), :] = (
                xp[8 * i:8 * i + 8, ls * LANES:(ls + 1) * LANES])
        for r in range(8):
            out_ref[0, r * group:(r + 1) * group, ls * LANES:(ls + 1) * LANES] = (
                perm_ref[ls, r * PERM_PITCH:r * PERM_PITCH + group, :])


def _tail_kernel(o_ref, g_ref, x_ref, gm_ref, wm_ref, wf_ref, wo_ref,
                 ng_ref, sc_ref, sh_ref, gf_ref, cw_ref, cb_ref, fg_ref, wup_ref, wdn_ref,
                 out_ref, tail_ref, act_ref, perm_ref, *, chunk):
    s_idx = pl.program_id(1)

    @pl.when(s_idx == 0)
    def _():
        tail_ref[...] = jnp.zeros_like(tail_ref)

    x1 = _interleave_rows(
        _mixer_residual(o_ref, g_ref, x_ref, gm_ref, wm_ref, wf_ref, wo_ref), perm_ref)
    tm = x1.shape[0]
    h = _modulated_norm(x1, ng_ref, sc_ref, sh_ref)
    first_row = lax.broadcasted_iota(jnp.int32, (8, chunk), 0) < 1

    def conv(c0):
        u = jnp.dot(h, wup_ref[:, c0:c0 + chunk], preferred_element_type=F32)
        prev = tail_ref[:, c0:c0 + chunk]
        tail_ref[:, c0:c0 + chunk] = u[tm - 16:tm, :]
        wrap = [jnp.where(first_row, pltpu.roll(prev[8 * i:8 * i + 8, :], 1, 0),
                          pltpu.roll(u[tm - 16 + 8 * i:tm - 8 + 8 * i, :], 1, 0)) for i in range(2)]
        back1 = jnp.concatenate([wrap[1], u[0:tm - 8, :]], axis=0)
        back2 = jnp.concatenate([wrap[0], wrap[1], u[0:tm - 16, :]], axis=0)
        w = cw_ref[:, c0:c0 + chunk]
        return cb_ref[:, c0:c0 + chunk] + w[0:1] * back2 + w[1:2] * back1 + w[2:3] * u

    for c0 in range(0, D_FF, chunk):
        act_ref[:, c0:c0 + chunk] = (_silu(conv(c0)) * conv(D_FF + c0)).astype(BF16)

    y = jnp.dot(act_ref[...], wdn_ref[...], preferred_element_type=F32)
    x2 = x1 + gf_ref[0] * y
    _store_natural_rows(_rms_scale(x2) * fg_ref[...], perm_ref, out_ref)


def _tail_call(o_all, g_all, x, g_m, wm, wf, wo, norm_g, sc_ffn, sh_ffn, g_f, conv_w, conv_b,
               final_g, wup, wdn, tm, chunk):
    b, s, d = x.shape
    const = _resident
    tok = lambda w: pl.BlockSpec((1, tm, w), lambda bi, si: (bi, si, 0))
    vec = pl.BlockSpec((1, 1, d), lambda bi, si: (bi, 0, 0))
    return pl.pallas_call(
        functools.partial(_tail_kernel, chunk=chunk),
        grid=(b, s // tm),
        in_specs=[tok(o_all.shape[-1]), tok(g_all.shape[-1]), tok(d), vec,
                  const(wm.shape), const(wf.shape), const(wo.shape),
                  const((1, d)), vec, vec, vec, const(conv_w.shape), const(conv_b.shape),
                  const(final_g.shape), const(wup.shape), const(wdn.shape)],
        out_specs=tok(d),
        out_shape=jax.ShapeDtypeStruct((b, s, d), F32),
        scratch_shapes=[pltpu.VMEM((16, 2 * D_FF), F32), pltpu.VMEM((tm, D_FF), BF16),
                        pltpu.VMEM((d // LANES, 8 * PERM_PITCH, LANES), F32)],
        compiler_params=pltpu.CompilerParams(dimension_semantics=("arbitrary", "arbitrary"),
                                             vmem_limit_bytes=VMEM_LIMIT),
        name="tail",
    )(o_all, g_all, x, g_m, wm, wf, wo, norm_g, sc_ffn, sh_ffn, g_f, conv_w, conv_b, final_g,
      wup, wdn)


def _pad_heads(w, heads, width):
    k = w.shape[0]
    w = w.reshape(k, heads, width)
    return jnp.pad(w, ((0, 0), (0, 0), (0, HEAD_PAD - width))).reshape(k, heads * HEAD_PAD)


def _prep_w_in(w_in):
    d = w_in.shape[0]
    splits = (MLA_Q_RANK, MLA_KV_RANK, MLA_ROPE, FOX_HEADS * FOX_DIM, FOX_HEADS * FOX_DIM,
              FOX_HEADS * FOX_DIM, FOX_HEADS, D_MODEL, D_MODEL)
    offs = [0]
    for n in splits:
        offs.append(offs[-1] + n)
    w_in = w_in.astype(BF16)
    piece = lambda i, j=None: w_in[:, offs[i]:offs[i + 1 if j is None else j]]
    misc = jnp.concatenate([jnp.zeros((d, MLA_NOPE), BF16), piece(2), piece(6),
                            jnp.zeros((d, LANES - FLOGIT_LO - FOX_HEADS), BF16)], axis=1)
    return [jnp.concatenate([piece(0, 2), misc], axis=1), piece(3, 6), piece(7, 9)]


def _bias_placement():
    part, head = np.meshgrid(np.arange(3), np.arange(FOX_HEADS), indexing="ij")
    rows = (FLOGIT_LO + part * FOX_HEADS + head).reshape(-1)
    q_cols = (head * HEAD_PAD + AUG_LO + part).reshape(-1)
    k_cols = (head * HEAD_PAD + AUG_LO + 3 + part).reshape(-1)
    width = FOX_HEADS * HEAD_PAD
    sq, sk = np.zeros((LANES, width), np.float32), np.zeros((LANES, width), np.float32)
    qc, kc = np.zeros((1, width), np.float32), np.zeros((1, width), np.float32)
    sq[rows, q_cols] = 1.0
    sk[rows, k_cols] = -1.0
    qc[0, k_cols] = 1.0
    kc[0, q_cols] = 1.0
    return jnp.asarray(sq, BF16), jnp.asarray(sk, BF16), jnp.asarray(qc), jnp.asarray(kc)


def kernel(x, c, positions, w_ada, b_ada, norm_mix_g, w_in, q_norm_g, w_uq, kv_norm_g, w_ukv,
           b_forget, w_o_mla, w_o_fox, w_out, norm_ffn_g, w_up, conv_w, conv_b, w_down,
           norm_final_g):
    assert w_ada.shape[0] == 1, "single-layer block"
    b, s, d = x.shape
    ada = _ada_call(c, w_ada[0], b_ada[0])
    sh_m, sc_m, g_m, sh_f, sc_f, g_f = (ada[:, i * d:(i + 1) * d].reshape(b, 1, d)
                                        for i in range(N_ADA))

    inv_freq = ROPE_THETA ** (-jnp.arange(0, MLA_ROPE, 2, dtype=F32) / MLA_ROPE)
    freq_row = jnp.concatenate([jnp.zeros((MLA_NOPE,), F32), inv_freq, inv_freq,
                                jnp.zeros((LANES - FLOGIT_LO,), F32)]).reshape(1, LANES)
    bf_row = jnp.concatenate([jnp.zeros((FLOGIT_LO,), F32), b_forget[0].astype(F32),
                              jnp.zeros((LANES - FLOGIT_LO - FOX_HEADS,), F32)]).reshape(1, LANES)
    pos_f = positions.astype(F32).reshape(b, 1, s)

    wuq = _pad_heads(w_uq[0], MLA_HEADS, MLA_NOPE + MLA_ROPE).astype(BF16)
    wukv = w_ukv[0].reshape(MLA_KV_RANK, MLA_HEADS, MLA_NOPE + MLA_V)
    wukv = jnp.concatenate([wukv[:, :, :MLA_NOPE].reshape(MLA_KV_RANK, -1),
                            wukv[:, :, MLA_NOPE:].reshape(MLA_KV_RANK, -1)], axis=1).astype(BF16)

    qt_all, k_all, vt_all, g_all = _inproj_call(
        x, pos_f, norm_mix_g[0].reshape(1, d), sc_m, sh_m, freq_row, bf_row,
        q_norm_g[0].reshape(1, -1), kv_norm_g[0].reshape(1, -1),
        _prep_w_in(w_in[0]) + [wuq, wukv] + list(_bias_placement()), tm=512)
    o_all = _attn_call(qt_all, k_all, vt_all, tq=256, nsub=8)
    return _tail_call(o_all, g_all, x, g_m, w_o_mla[0].astype(BF16), w_o_fox[0].astype(BF16),
                      w_out[0].astype(BF16), norm_ffn_g[0].reshape(1, d), sc_f, sh_f, g_f, conv_w[0],
                      conv_b[0].reshape(1, -1), norm_final_g.reshape(1, -1),
                      w_up[0].astype(BF16), w_down[0].astype(BF16), tm=512, chunk=256)
```

```python
import functools
import math

import jax
import jax.numpy as jnp
import numpy as np
from jax import lax
from jax.experimental import pallas as pl
from jax.experimental.pallas import tpu as pltpu

F32 = jnp.float32
BF16 = jnp.bfloat16

D_MODEL = 1024
MLA_HEADS = 8
MLA_Q_RANK = 384
MLA_KV_RANK = 256
MLA_NOPE = 64
MLA_ROPE = 32
MLA_V = 64
MLA_SCALE = 1.0 / math.sqrt(MLA_NOPE + MLA_ROPE)
ROPE_THETA = 10000.0
FOX_HEADS = 8
FOX_DIM = 64
FOX_SCALE = 1.0 / math.sqrt(FOX_DIM)
LOG2E = math.log2(math.e)
D_FF = 2816
CONV_WIDTH = 3
EPS = 1e-6
N_ADA = 6

LANES = 128
HEAD_PAD = LANES
N_HEADS = MLA_HEADS + FOX_HEADS
ROPE_HALF = MLA_ROPE // 2
ROPE_LO = MLA_NOPE
ROPE_HI = MLA_NOPE + ROPE_HALF
FLOGIT_LO = MLA_NOPE + MLA_ROPE
AUG_LO = FOX_DIM
ONES_ROWS = 16

VMEM_LIMIT = 56 * 1024 * 1024

ADA_COLS = 1536
INPROJ_ROWS = 512
TAIL_ROWS = 512
FFN_CHUNK = 256
ATTN_TILE = 256
ATTN_SUBBLOCKS = 8
PERM_PITCH = TAIL_ROWS // 8 + 8


def _silu(x):
    return x * (1.0 / (1.0 + jnp.exp(-x)))


def _sigmoid(x):
    return 1.0 / (1.0 + jnp.exp(-x))


def _rms_scale(x):
    return x * lax.rsqrt(jnp.mean(x * x, axis=-1, keepdims=True) + EPS)


def _ada_kernel(ct_ref, w_ref, b_ref, o_ref, *, batch):
    act = _silu(ct_ref[...])
    w = w_ref[...]
    for b in range(batch):
        o_ref[b:b + 1, :] = jnp.sum(w * act[:, b:b + 1], axis=0, keepdims=True) + b_ref[...]


def _ada_call(c, w_ada, b_ada):
    batch, d = c.shape
    n = w_ada.shape[1]
    tn = ADA_COLS
    return pl.pallas_call(
        functools.partial(_ada_kernel, batch=batch),
        grid=(n // tn,),
        in_specs=[pl.BlockSpec((d, batch), lambda j: (0, 0)),
                  pl.BlockSpec((d, tn), lambda j: (0, j)),
                  pl.BlockSpec((1, tn), lambda j: (0, j))],
        out_specs=pl.BlockSpec((batch, tn), lambda j: (0, j)),
        out_shape=jax.ShapeDtypeStruct((batch, n), F32),
        compiler_params=pltpu.CompilerParams(dimension_semantics=("arbitrary",),
                                             vmem_limit_bytes=VMEM_LIMIT),
        name="ada",
    )(c.T, w_ada, b_ada.reshape(1, n))


def _rope_tables(pos_col, freq_row):
    ang = pos_col * freq_row
    lane = lax.broadcasted_iota(jnp.int32, ang.shape, 1)
    cos = jnp.cos(ang)
    sin = jnp.sin(ang)
    s_lo = jnp.where((lane >= ROPE_LO) & (lane < ROPE_HI), -sin, 0.0)
    s_hi = jnp.where((lane >= ROPE_HI) & (lane < FLOGIT_LO), sin, 0.0)
    return cos, s_lo, s_hi


def _rope_block(t, cos, s_lo, s_hi):
    return (t * cos + pltpu.roll(t, LANES - ROPE_HALF, 1) * s_lo
            + pltpu.roll(t, ROPE_HALF, 1) * s_hi)


def _split3(x):
    hi = x.astype(BF16)
    r1 = x - hi.astype(F32)
    mid = r1.astype(BF16)
    lo = (r1 - mid.astype(F32)).astype(BF16)
    return hi, mid, lo


def _pack3(x):
    hi, mid, lo = (p.astype(F32) for p in _split3(x))
    return (hi + pltpu.roll(mid, FOX_HEADS, 1) + pltpu.roll(lo, 2 * FOX_HEADS, 1)).astype(BF16)


def _modulated_norm(x, gain_ref, sc_ref, sh_ref):
    return (_rms_scale(x) * (gain_ref[...] * (1.0 + sc_ref[0])) + sh_ref[0]).astype(BF16)


def _inproj_kernel(x_ref, pos_ref, ng_ref, sc_ref, sh_ref, freq_ref, bf_ref, qg_ref, kvg_ref,
                   wlat_ref, wfox_ref, wgate_ref,
                   wuq_ref, wukv_ref, sq_ref, sk_ref, qc_ref, kc_ref,
                   qt_ref, k_ref, vt_ref, g_ref, carry_ref, *, tm):
    s_idx = pl.program_id(1)

    @pl.when(s_idx == 0)
    def _():
        carry_ref[...] = jnp.zeros_like(carry_ref)

    h = _modulated_norm(x_ref[0], ng_ref, sc_ref, sh_ref)

    def proj(w_ref, lo=None, hi=None):
        w = w_ref[...] if lo is None else w_ref[:, lo:hi]
        return jnp.dot(h, w, preferred_element_type=F32)

    lane = lax.broadcasted_iota(jnp.int32, (tm, LANES), 1)
    pos_col = jnp.broadcast_to(pos_ref[0], (8, tm)).T[:, 0:1]
    cos, s_lo, s_hi = _rope_tables(pos_col, freq_ref[...])

    lat = proj(wlat_ref)
    lat_q = lat[:, 0:MLA_Q_RANK]
    lat_kv = lat[:, MLA_Q_RANK:MLA_Q_RANK + MLA_KV_RANK]
    misc = lat[:, MLA_Q_RANK + MLA_KV_RANK:]
    g_ref[0, :, 0:D_MODEL] = _sigmoid(proj(wgate_ref, 0, D_MODEL)).astype(BF16)

    cqn = (_rms_scale(lat_q) * qg_ref[...]).astype(BF16)
    ckvn = (_rms_scale(lat_kv) * kvg_ref[...]).astype(BF16)
    qm = jnp.dot(cqn, wuq_ref[...], preferred_element_type=F32)
    kvm = jnp.dot(ckvn, wukv_ref[...], preferred_element_type=F32)
    kn, vm = kvm[:, 0:MLA_HEADS * MLA_NOPE], kvm[:, MLA_HEADS * MLA_NOPE:]
    fv = proj(wfox_ref, 2 * FOX_HEADS * FOX_DIM, 3 * FOX_HEADS * FOX_DIM)
    for hd in range(MLA_HEADS):
        blk = qm[:, hd * HEAD_PAD:(hd + 1) * HEAD_PAD]
        qt_ref[0, hd * HEAD_PAD:(hd + 1) * HEAD_PAD, :] = (
            _rope_block(blk, cos, s_lo, s_hi) * (MLA_SCALE * LOG2E)).T.astype(BF16)

    krope = _rope_block(jnp.where(lane < FLOGIT_LO, misc, 0.0), cos, s_lo, s_hi)
    for hd in range(MLA_HEADS):
        kb = kn[:, (hd // 2) * LANES:(hd // 2 + 1) * LANES]
        if hd % 2:
            kb = pltpu.roll(kb, MLA_NOPE, 1)
        k_ref[0, :, hd * HEAD_PAD:(hd + 1) * HEAD_PAD] = jnp.where(
            lane < MLA_NOPE, kb, krope).astype(BF16)
    vt_ref[0, 0:MLA_HEADS * MLA_V, :] = vm.T.astype(BF16)
    vt_ref[0, MLA_HEADS * MLA_V:, :] = fv.T.astype(BF16)

    g_ref[0, :, D_MODEL:] = _sigmoid(proj(wgate_ref, D_MODEL, 2 * D_MODEL)).astype(BF16)

    z = misc + bf_ref[...]
    logf = jnp.minimum(z, 0.0) - jnp.log1p(jnp.exp(-jnp.abs(z)))
    logf = jnp.where((lane >= FLOGIT_LO) & (lane < FLOGIT_LO + FOX_HEADS), logf, 0.0)
    row = lax.broadcasted_iota(jnp.int32, (tm, tm), 0)
    col = lax.broadcasted_iota(jnp.int32, (tm, tm), 1)
    tri = jnp.where(col <= row, 1.0, 0.0).astype(BF16)
    parts = jnp.dot(tri, _pack3(logf), preferred_element_type=F32)
    total = (parts + pltpu.roll(parts, LANES - FOX_HEADS, 1)
             + pltpu.roll(parts, LANES - 2 * FOX_HEADS, 1))
    fcum = carry_ref[...] + jnp.where(
        (lane >= FLOGIT_LO) & (lane < FLOGIT_LO + FOX_HEADS), total, 0.0)
    carry_ref[...] = fcum[tm - 1:tm, :]
    fq = proj(wfox_ref, 0, FOX_HEADS * FOX_DIM) * (FOX_SCALE * LOG2E)
    fk = proj(wfox_ref, FOX_HEADS * FOX_DIM, 2 * FOX_HEADS * FOX_DIM)

    bias3 = _pack3(fcum * LOG2E)
    aug_q = jnp.dot(bias3, sq_ref[...], preferred_element_type=F32) + qc_ref[...]
    aug_k = jnp.dot(bias3, sk_ref[...], preferred_element_type=F32) + kc_ref[...]
    for hd in range(FOX_HEADS):
        pair = slice((hd // 2) * LANES, (hd // 2 + 1) * LANES)
        blk = slice(hd * HEAD_PAD, (hd + 1) * HEAD_PAD)
        qb, kb = fq[:, pair], fk[:, pair]
        if hd % 2:
            qb, kb = pltpu.roll(qb, FOX_DIM, 1), pltpu.roll(kb, FOX_DIM, 1)
        c0 = (MLA_HEADS + hd) * HEAD_PAD
        qt_ref[0, c0:c0 + HEAD_PAD, :] = jnp.where(lane < FOX_DIM, qb, aug_q[:, blk]).T.astype(BF16)
        k_ref[0, :, c0:c0 + HEAD_PAD] = jnp.where(lane < FOX_DIM, kb, aug_k[:, blk]).astype(BF16)


def _resident(shape):
    return pl.BlockSpec(shape, lambda bi, si: (0,) * len(shape), pipeline_mode=pl.Buffered(1))


def _inproj_call(x, pos_f, norm_g, sc_mix, sh_mix, freq_row, bf_row, qg, kvg, weights, tm):
    b, s, d = x.shape
    tok = lambda w: pl.BlockSpec((1, tm, w), lambda bi, si: (bi, si, 0))
    vec = pl.BlockSpec((1, 1, d), lambda bi, si: (bi, 0, 0))
    return pl.pallas_call(
        functools.partial(_inproj_kernel, tm=tm),
        grid=(b, s // tm),
        in_specs=[tok(d), pl.BlockSpec((1, 1, tm), lambda bi, si: (bi, 0, si)),
                  _resident((1, d)), vec, vec, _resident((1, LANES)),
                  _resident((1, LANES)), _resident((1, MLA_Q_RANK)), _resident((1, MLA_KV_RANK))]
                 + [_resident(w.shape) for w in weights],
        out_specs=[pl.BlockSpec((1, N_HEADS * HEAD_PAD, tm), lambda bi, si: (bi, 0, si)),
                   tok(N_HEADS * HEAD_PAD),
                   pl.BlockSpec((1, N_HEADS * FOX_DIM, tm), lambda bi, si: (bi, 0, si)),
                   tok(2 * d)],
        out_shape=[jax.ShapeDtypeStruct((b, N_HEADS * HEAD_PAD, s), BF16),
                   jax.ShapeDtypeStruct((b, s, N_HEADS * HEAD_PAD), BF16),
                   jax.ShapeDtypeStruct((b, N_HEADS * FOX_DIM, s), BF16),
                   jax.ShapeDtypeStruct((b, s, 2 * d), BF16)],
        scratch_shapes=[pltpu.VMEM((1, LANES), F32)],
        compiler_params=pltpu.CompilerParams(dimension_semantics=("arbitrary", "arbitrary"),
                                             vmem_limit_bytes=VMEM_LIMIT),
        name="inproj",
    )(x, pos_f, norm_g, sc_mix, sh_mix, freq_row, bf_row, qg, kvg, *weights)


def _attn_kernel(qt_ref, k_ref, vt_ref, o_ref, st_ref, mx_ref, m_ref, acc_ref, *, tq, tk, nsub):
    qi = pl.program_id(2)
    assert tq == tk
    row = lax.broadcasted_iota(jnp.int32, (tk, tq), 0)
    col = lax.broadcasted_iota(jnp.int32, (tk, tq), 1)
    causal = row <= col
    ones_rows = jnp.ones((ONES_ROWS, tk), BF16)
    streams = [(hh, sub) for hh in range(2) for sub in range(nsub)]

    def scores(idx, j, slot, full):
        hh, sub = streams[idx]
        start = pl.multiple_of(j * tk, tk)
        qt = qt_ref[0, hh * HEAD_PAD:(hh + 1) * HEAD_PAD, sub * tq:(sub + 1) * tq]
        k = k_ref[0, pl.ds(start, tk), hh * HEAD_PAD:(hh + 1) * HEAD_PAD]
        st = jnp.dot(k, qt, preferred_element_type=F32)
        if full is False:
            st = jnp.where(causal, st, -jnp.inf)
        elif full is not True:
            st = jnp.where(row <= col + jnp.where(full, tk, 0), st, -jnp.inf)
        st_ref[slot, idx] = st
        mx_ref[slot, idx] = jnp.max(st, axis=0, keepdims=True)

    def update(idx, j, slot):
        hh, _ = streams[idx]
        start = pl.multiple_of(j * tk, tk)
        m = m_ref[idx]
        m_new = jnp.maximum(m, mx_ref[slot, idx])
        m_ref[idx] = m_new
        p = jnp.exp2(st_ref[slot, idx] - m_new).astype(BF16)
        vt = vt_ref[0, hh * MLA_V:(hh + 1) * MLA_V, pl.ds(start, tk)]
        vt_aug = jnp.concatenate([vt, ones_rows], axis=0)
        acc_ref[idx] = (jnp.exp2(m - m_new) * acc_ref[idx]
                        + jnp.dot(vt_aug, p, preferred_element_type=F32))

    def step(j, slot, diag_sub=None, next_is_full=None):
        first = 0 if diag_sub is None else diag_sub
        for idx, (_, sub) in enumerate(streams):
            if diag_sub is None:
                scores(idx, j + 1, 1 - slot, full=next_is_full if sub == 0 else True)
            elif sub > first:
                scores(idx, j + 1, 1 - slot, full=(sub != diag_sub + 1))
            if sub >= first:
                update(idx, j, slot)

    m_ref[...] = jnp.full(m_ref.shape, -jnp.inf, F32)
    acc_ref[...] = jnp.zeros(acc_ref.shape, F32)
    assert nsub % 2 == 0
    n_full = qi * nsub
    for idx, (_, sub) in enumerate(streams):
        scores(idx, 0, 0, full=(qi > 0) if sub == 0 else True)

    def body(i, carry):
        for u in range(nsub):
            step(nsub * i + u, u % 2, next_is_full=True if u < nsub - 1 else i < qi - 1)
        return carry

    lax.fori_loop(0, qi, body, 0)
    for r in range(nsub):
        step(n_full + r, r % 2, diag_sub=r)
        outs = []
        for hh in range(2):
            acc = acc_ref[streams.index((hh, r))]
            outs.append(acc[0:MLA_V, :] / acc[MLA_V:MLA_V + 1, :])
        o_ref[0, r * tq:(r + 1) * tq, :] = jnp.concatenate(outs, axis=0).T.astype(BF16)


def _attn_call(qt_all, k_all, vt_all, tq, nsub):
    b, s, _ = k_all.shape
    pairs = N_HEADS // 2
    bq = tq * nsub
    return pl.pallas_call(
        functools.partial(_attn_kernel, tq=tq, tk=tq, nsub=nsub),
        grid=(b, pairs, s // bq),
        in_specs=[pl.BlockSpec((1, 2 * HEAD_PAD, bq), lambda bi, pi, qi: (bi, pi, qi)),
                  pl.BlockSpec((1, s, 2 * HEAD_PAD), lambda bi, pi, qi: (bi, 0, pi)),
                  pl.BlockSpec((1, LANES, s), lambda bi, pi, qi: (bi, pi, 0))],
        out_specs=pl.BlockSpec((1, bq, LANES), lambda bi, pi, qi: (bi, qi, pi)),
        out_shape=jax.ShapeDtypeStruct((b, s, pairs * LANES), BF16),
        scratch_shapes=[pltpu.VMEM((2, 2 * nsub, tq, tq), F32),
                        pltpu.VMEM((2, 2 * nsub, 1, tq), F32),
                        pltpu.VMEM((2 * nsub, 1, tq), F32),
                        pltpu.VMEM((2 * nsub, MLA_V + ONES_ROWS, tq), F32)],
        compiler_params=pltpu.CompilerParams(
            dimension_semantics=("arbitrary", "arbitrary", "arbitrary"),
            vmem_limit_bytes=VMEM_LIMIT),
        name="attn",
    )(qt_all, k_all, vt_all)


def _mixer_residual(o_ref, g_ref, x_ref, gm_ref, wm_ref, wf_ref, wo_ref):
    half = MLA_HEADS * MLA_V
    o = o_ref[0]
    g = g_ref[0].astype(F32)
    y = (g[:, :D_MODEL] * jnp.dot(o[:, :half], wm_ref[...], preferred_element_type=F32)
         + g[:, D_MODEL:] * jnp.dot(o[:, half:], wf_ref[...], preferred_element_type=F32))
    mix = jnp.dot(y.astype(BF16), wo_ref[...], preferred_element_type=F32)
    return x_ref[0] + gm_ref[0] * mix


def _interleave_rows(x, perm_ref):
    tm, d = x.shape
    group = tm // 8
    for ls in range(d // LANES):
        for r in range(8):
            perm_ref[ls, r * PERM_PITCH:r * PERM_PITCH + group, :] = (
                x[r * group:(r + 1) * group, ls * LANES:(ls + 1) * LANES])
    return jnp.concatenate(
        [jnp.concatenate([perm_ref[ls, pl.ds(i, 8, stride=PERM_PITCH), :] for i in range(group)],
                         axis=0) for ls in range(d // LANES)], axis=1)


def _store_natural_rows(xp, perm_ref, out_ref):
    tm, d = xp.shape
    group = tm // 8
    for ls in range(d // LANES):
        for i in range(group):
            perm_ref[ls, pl.ds(i, 8, stride=PERM_PITCH), :] = (
                xp[8 * i:8 * i + 8, ls * LANES:(ls + 1) * LANES])
        for r in range(8):
            out_ref[0, r * group:(r + 1) * group, ls * LANES:(ls + 1) * LANES] = (
                perm_ref[ls, r * PERM_PITCH:r * PERM_PITCH + group, :])


def _tail_kernel(o_ref, g_ref, x_ref, gm_ref, wm_ref, wf_ref, wo_ref,
                 ng_ref, sc_ref, sh_ref, gf_ref, cw_ref, cb_ref, fg_ref, wup_ref, wdn_ref,
                 out_ref, tail_ref, act_ref, perm_ref, *, chunk):
    s_idx = pl.program_id(1)

    @pl.when(s_idx == 0)
    def _():
        tail_ref[...] = jnp.zeros_like(tail_ref)

    x1 = _interleave_rows(
        _mixer_residual(o_ref, g_ref, x_ref, gm_ref, wm_ref, wf_ref, wo_ref), perm_ref)
    tm = x1.shape[0]
    h = _modulated_norm(x1, ng_ref, sc_ref, sh_ref)
    first_row = lax.broadcasted_iota(jnp.int32, (8, chunk), 0) < 1

    def conv(c0):
        u = jnp.dot(h, wup_ref[:, c0:c0 + chunk], preferred_element_type=F32)
        prev = tail_ref[:, c0:c0 + chunk]
        tail_ref[:, c0:c0 + chunk] = u[tm - 16:tm, :]
        wrap = [jnp.where(first_row, pltpu.roll(prev[8 * i:8 * i + 8, :], 1, 0),
                          pltpu.roll(u[tm - 16 + 8 * i:tm - 8 + 8 * i, :], 1, 0)) for i in range(2)]
        back1 = jnp.concatenate([wrap[1], u[0:tm - 8, :]], axis=0)
        back2 = jnp.concatenate([wrap[0], wrap[1], u[0:tm - 16, :]], axis=0)
        w = cw_ref[:, c0:c0 + chunk]
        return cb_ref[:, c0:c0 + chunk] + w[0:1] * back2 + w[1:2] * back1 + w[2:3] * u

    for c0 in range(0, D_FF, chunk):
        act_ref[:, c0:c0 + chunk] = (_silu(conv(c0)) * conv(D_FF + c0)).astype(BF16)

    y = jnp.dot(act_ref[...], wdn_ref[...], preferred_element_type=F32)
    x2 = x1 + gf_ref[0] * y
    _store_natural_rows(_rms_scale(x2) * fg_ref[...], perm_ref, out_ref)


def _tail_call(o_all, g_all, x, g_m, wm, wf, wo, norm_g, sc_ffn, sh_ffn, g_f, conv_w, conv_b,
               final_g, wup, wdn, tm, chunk):
    b, s, d = x.shape
    const = _resident
    tok = lambda w: pl.BlockSpec((1, tm, w), lambda bi, si: (bi, si, 0))
    vec = pl.BlockSpec((1, 1, d), lambda bi, si: (bi, 0, 0))
    return pl.pallas_call(
        functools.partial(_tail_kernel, chunk=chunk),
        grid=(b, s // tm),
        in_specs=[tok(o_all.shape[-1]), tok(g_all.shape[-1]), tok(d), vec,
                  const(wm.shape), const(wf.shape), const(wo.shape),
                  const((1, d)), vec, vec, vec, const(conv_w.shape), const(conv_b.shape),
                  const(final_g.shape), const(wup.shape), const(wdn.shape)],
        out_specs=tok(d),
        out_shape=jax.ShapeDtypeStruct((b, s, d), F32),
        scratch_shapes=[pltpu.VMEM((16, 2 * D_FF), F32), pltpu.VMEM((tm, D_FF), BF16),
                        pltpu.VMEM((d // LANES, 8 * PERM_PITCH, LANES), F32)],
        compiler_params=pltpu.CompilerParams(dimension_semantics=("arbitrary", "arbitrary"),
                                             vmem_limit_bytes=VMEM_LIMIT),
        name="tail",
    )(o_all, g_all, x, g_m, wm, wf, wo, norm_g, sc_ffn, sh_ffn, g_f, conv_w, conv_b, final_g,
      wup, wdn)


def _pad_heads(w, heads, width):
    k = w.shape[0]
    w = w.reshape(k, heads, width)
    return jnp.pad(w, ((0, 0), (0, 0), (0, HEAD_PAD - width))).reshape(k, heads * HEAD_PAD)


def _prep_w_in(w_in):
    d = w_in.shape[0]
    splits = (MLA_Q_RANK, MLA_KV_RANK, MLA_ROPE, FOX_HEADS * FOX_DIM, FOX_HEADS * FOX_DIM,
              FOX_HEADS * FOX_DIM, FOX_HEADS, D_MODEL, D_MODEL)
    offs = [0]
    for n in splits:
        offs.append(offs[-1] + n)
    w_in = w_in.astype(BF16)
    piece = lambda i, j=None: w_in[:, offs[i]:offs[i + 1 if j is None else j]]
    misc = jnp.concatenate([jnp.zeros((d, MLA_NOPE), BF16), piece(2), piece(6),
                            jnp.zeros((d, LANES - FLOGIT_LO - FOX_HEADS), BF16)], axis=1)
    return [jnp.concatenate([piece(0, 2), misc], axis=1), piece(3, 6), piece(7, 9)]


def _bias_placement():
    part, head = np.meshgrid(np.arange(3), np.arange(FOX_HEADS), indexing="ij")
    rows = (FLOGIT_LO + part * FOX_HEADS + head).reshape(-1)
    q_cols = (head * HEAD_PAD + AUG_LO + part).reshape(-1)
    k_cols = (head * HEAD_PAD + AUG_LO + 3 + part).reshape(-1)
    width = FOX_HEADS * HEAD_PAD
    sq, sk = np.zeros((LANES, width), np.float32), np.zeros((LANES, width), np.float32)
    qc, kc = np.zeros((1, width), np.float32), np.zeros((1, width), np.float32)
    sq[rows, q_cols] = 1.0
    sk[rows, k_cols] = -1.0
    qc[0, k_cols] = 1.0
    kc[0, q_cols] = 1.0
    return jnp.asarray(sq, BF16), jnp.asarray(sk, BF16), jnp.asarray(qc), jnp.asarray(kc)


def kernel(x, c, positions, w_ada, b_ada, norm_mix_g, w_in, q_norm_g, w_uq, kv_norm_g, w_ukv,
           b_forget, w_o_mla, w_o_fox, w_out, norm_ffn_g, w_up, conv_w, conv_b, w_down,
           norm_final_g):
    assert w_ada.shape[0] == 1, "single-layer block"
    b, s, d = x.shape
    ada = _ada_call(c, w_ada[0], b_ada[0])
    sh_m, sc_m, g_m, sh_f, sc_f, g_f = (ada[:, i * d:(i + 1) * d].reshape(b, 1, d)
                                        for i in range(N_ADA))

    inv_freq = ROPE_THETA ** (-jnp.arange(0, MLA_ROPE, 2, dtype=F32) / MLA_ROPE)
    freq_row = jnp.concatenate([jnp.zeros((MLA_NOPE,), F32), inv_freq, inv_freq,
                                jnp.zeros((LANES - FLOGIT_LO,), F32)]).reshape(1, LANES)
    bf_row = jnp.concatenate([jnp.zeros((FLOGIT_LO,), F32), b_forget[0].astype(F32),
                              jnp.zeros((LANES - FLOGIT_LO - FOX_HEADS,), F32)]).reshape(1, LANES)
    pos_f = positions.astype(F32).reshape(b, 1, s)

    wuq = _pad_heads(w_uq[0], MLA_HEADS, MLA_NOPE + MLA_ROPE).astype(BF16)
    wukv = w_ukv[0].reshape(MLA_KV_RANK, MLA_HEADS, MLA_NOPE + MLA_V)
    wukv = jnp.concatenate([wukv[:, :, :MLA_NOPE].reshape(MLA_KV_RANK, -1),
                            wukv[:, :, MLA_NOPE:].reshape(MLA_KV_RANK, -1)], axis=1).astype(BF16)

    qt_all, k_all, vt_all, g_all = _inproj_call(
        x, pos_f, norm_mix_g[0].reshape(1, d), sc_m, sh_m, freq_row, bf_row,
        q_norm_g[0].reshape(1, -1), kv_norm_g[0].reshape(1, -1),
        _prep_w_in(w_in[0]) + [wuq, wukv] + list(_bias_placement()), tm=INPROJ_ROWS)
    o_all = _attn_call(qt_all, k_all, vt_all, tq=ATTN_TILE, nsub=ATTN_SUBBLOCKS)
    return _tail_call(o_all, g_all, x, g_m, w_o_mla[0].astype(BF16), w_o_fox[0].astype(BF16),
                      w_out[0].astype(BF16), norm_ffn_g[0].reshape(1, d), sc_f, sh_f, g_f, conv_w[0],
                      conv_b[0].reshape(1, -1), norm_final_g.reshape(1, -1),
                      w_up[0].astype(BF16), w_down[0].astype(BF16), tm=TAIL_ROWS,
                      chunk=FFN_CHUNK)
```

```python
import functools
import math

import jax
import jax.numpy as jnp
import numpy as np
from jax import lax
from jax.experimental import pallas as pl
from jax.experimental.pallas import tpu as pltpu

F32 = jnp.float32
BF16 = jnp.bfloat16

D_MODEL = 1024
MLA_HEADS = 8
MLA_Q_RANK = 384
MLA_KV_RANK = 256
MLA_NOPE = 64
MLA_ROPE = 32
MLA_V = 64
MLA_SCALE = 1.0 / math.sqrt(MLA_NOPE + MLA_ROPE)
ROPE_THETA = 10000.0
FOX_HEADS = 8
FOX_DIM = 64
FOX_SCALE = 1.0 / math.sqrt(FOX_DIM)
LOG2E = math.log2(math.e)
D_FF = 2816
CONV_WIDTH = 3
EPS = 1e-6
N_ADA = 6

LANES = 128
HEAD_PAD = LANES
N_HEADS = MLA_HEADS + FOX_HEADS
ROPE_HALF = MLA_ROPE // 2
ROPE_LO = MLA_NOPE
ROPE_HI = MLA_NOPE + ROPE_HALF
FLOGIT_LO = MLA_NOPE + MLA_ROPE
AUG_LO = FOX_DIM
ONES_ROWS = 16

VMEM_LIMIT = 56 * 1024 * 1024

ADA_COLS = 1536
INPROJ_ROWS = 512
TAIL_ROWS = 512
FFN_CHUNK = 256
ATTN_TILE = 256
ATTN_SUBBLOCKS = 8
PERM_PITCH = TAIL_ROWS // 8 + 8


def _silu(x):
    return x * (1.0 / (1.0 + jnp.exp(-x)))


def _sigmoid(x):
    return 1.0 / (1.0 + jnp.exp(-x))


def _rms_scale(x):
    return x * lax.rsqrt(jnp.mean(x * x, axis=-1, keepdims=True) + EPS)


def _ada_kernel(ct_ref, w_ref, b_ref, o_ref, *, batch):
    act = _silu(ct_ref[...])
    w = w_ref[...]
    for b in range(batch):
        o_ref[b:b + 1, :] = jnp.sum(w * act[:, b:b + 1], axis=0, keepdims=True) + b_ref[...]


def _ada_call(c, w_ada, b_ada):
    batch, d = c.shape
    n = w_ada.shape[1]
    tn = ADA_COLS
    return pl.pallas_call(
        functools.partial(_ada_kernel, batch=batch),
        grid=(n // tn,),
        in_specs=[pl.BlockSpec((d, batch), lambda j: (0, 0)),
                  pl.BlockSpec((d, tn), lambda j: (0, j)),
                  pl.BlockSpec((1, tn), lambda j: (0, j))],
        out_specs=pl.BlockSpec((batch, tn), lambda j: (0, j)),
        out_shape=jax.ShapeDtypeStruct((batch, n), F32),
        compiler_params=pltpu.CompilerParams(dimension_semantics=("arbitrary",),
                                             vmem_limit_bytes=VMEM_LIMIT),
        name="ada",
    )(c.T, w_ada, b_ada.reshape(1, n))


def _rope_tables(pos_col, freq_row):
    ang = pos_col * freq_row
    lane = lax.broadcasted_iota(jnp.int32, ang.shape, 1)
    cos = jnp.cos(ang)
    sin = jnp.sin(ang)
    s_lo = jnp.where((lane >= ROPE_LO) & (lane < ROPE_HI), -sin, 0.0)
    s_hi = jnp.where((lane >= ROPE_HI) & (lane < FLOGIT_LO), sin, 0.0)
    return cos, s_lo, s_hi


def _rope_block(t, cos, s_lo, s_hi):
    return (t * cos + pltpu.roll(t, LANES - ROPE_HALF, 1) * s_lo
            + pltpu.roll(t, ROPE_HALF, 1) * s_hi)


def _split3(x):
    hi = x.astype(BF16)
    r1 = x - hi.astype(F32)
    mid = r1.astype(BF16)
    lo = (r1 - mid.astype(F32)).astype(BF16)
    return hi, mid, lo


def _pack3(x):
    hi, mid, lo = (p.astype(F32) for p in _split3(x))
    return (hi + pltpu.roll(mid, FOX_HEADS, 1) + pltpu.roll(lo, 2 * FOX_HEADS, 1)).astype(BF16)


def _modulated_norm(x, gain_ref, sc_ref, sh_ref):
    return (_rms_scale(x) * (gain_ref[...] * (1.0 + sc_ref[0])) + sh_ref[0]).astype(BF16)


def _inproj_kernel(x_ref, pos_ref, ng_ref, sc_ref, sh_ref, freq_ref, bf_ref, qg_ref, kvg_ref,
                   wlat_ref, wfox_ref, wgate_ref,
                   wuq_ref, wukv_ref, sq_ref, sk_ref, qc_ref, kc_ref,
                   qt_ref, k_ref, vt_ref, g_ref, carry_ref, *, tm):
    s_idx = pl.program_id(1)

    @pl.when(s_idx == 0)
    def _():
        carry_ref[...] = jnp.zeros_like(carry_ref)

    h = _modulated_norm(x_ref[0], ng_ref, sc_ref, sh_ref)

    def proj(w_ref, lo=None, hi=None):
        w = w_ref[...] if lo is None else w_ref[:, lo:hi]
        return jnp.dot(h, w, preferred_element_type=F32)

    lane = lax.broadcasted_iota(jnp.int32, (tm, LANES), 1)
    pos_col = jnp.broadcast_to(pos_ref[0], (8, tm)).T[:, 0:1]
    cos, s_lo, s_hi = _rope_tables(pos_col, freq_ref[...])

    lat = proj(wlat_ref)
    lat_q = lat[:, 0:MLA_Q_RANK]
    lat_kv = lat[:, MLA_Q_RANK:MLA_Q_RANK + MLA_KV_RANK]
    misc = lat[:, MLA_Q_RANK + MLA_KV_RANK:]
    g_ref[0, :, 0:D_MODEL] = _sigmoid(proj(wgate_ref, 0, D_MODEL)).astype(BF16)

    cqn = (_rms_scale(lat_q) * qg_ref[...]).astype(BF16)
    ckvn = (_rms_scale(lat_kv) * kvg_ref[...]).astype(BF16)
    qm = jnp.dot(cqn, wuq_ref[...], preferred_element_type=F32)
    kvm = jnp.dot(ckvn, wukv_ref[...], preferred_element_type=F32)
    kn, vm = kvm[:, 0:MLA_HEADS * MLA_NOPE], kvm[:, MLA_HEADS * MLA_NOPE:]
    fv = proj(wfox_ref, 2 * FOX_HEADS * FOX_DIM, 3 * FOX_HEADS * FOX_DIM)
    for hd in range(MLA_HEADS):
        blk = qm[:, hd * HEAD_PAD:(hd + 1) * HEAD_PAD]
        qt_ref[0, hd * HEAD_PAD:(hd + 1) * HEAD_PAD, :] = (
            _rope_block(blk, cos, s_lo, s_hi) * (MLA_SCALE * LOG2E)).T.astype(BF16)

    krope = _rope_block(jnp.where(lane < FLOGIT_LO, misc, 0.0), cos, s_lo, s_hi)
    for hd in range(MLA_HEADS):
        kb = kn[:, (hd // 2) * LANES:(hd // 2 + 1) * LANES]
        if hd % 2:
            kb = pltpu.roll(kb, MLA_NOPE, 1)
        k_ref[0, :, hd * HEAD_PAD:(hd + 1) * HEAD_PAD] = jnp.where(
            lane < MLA_NOPE, kb, krope).astype(BF16)
    vt_ref[0, 0:MLA_HEADS * MLA_V, :] = vm.T.astype(BF16)
    vt_ref[0, MLA_HEADS * MLA_V:, :] = fv.T.astype(BF16)

    g_ref[0, :, D_MODEL:] = _sigmoid(proj(wgate_ref, D_MODEL, 2 * D_MODEL)).astype(BF16)

    z = misc + bf_ref[...]
    logf = jnp.minimum(z, 0.0) - jnp.log1p(jnp.exp(-jnp.abs(z)))
    logf = jnp.where((lane >= FLOGIT_LO) & (lane < FLOGIT_LO + FOX_HEADS), logf, 0.0)
    row = lax.broadcasted_iota(jnp.int32, (tm, tm), 0)
    col = lax.broadcasted_iota(jnp.int32, (tm, tm), 1)
    tri = jnp.where(col <= row, 1.0, 0.0).astype(BF16)
    parts = jnp.dot(tri, _pack3(logf), preferred_element_type=F32)
    total = (parts + pltpu.roll(parts, LANES - FOX_HEADS, 1)
             + pltpu.roll(parts, LANES - 2 * FOX_HEADS, 1))
    fcum = carry_ref[...] + jnp.where(
        (lane >= FLOGIT_LO) & (lane < FLOGIT_LO + FOX_HEADS), total, 0.0)
    carry_ref[...] = fcum[tm - 1:tm, :]
    fq = proj(wfox_ref, 0, FOX_HEADS * FOX_DIM) * (FOX_SCALE * LOG2E)
    fk = proj(wfox_ref, FOX_HEADS * FOX_DIM, 2 * FOX_HEADS * FOX_DIM)

    bias3 = _pack3(fcum * LOG2E)
    aug_q = jnp.dot(bias3, sq_ref[...], preferred_element_type=F32) + qc_ref[...]
    aug_k = jnp.dot(bias3, sk_ref[...], preferred_element_type=F32) + kc_ref[...]
    for hd in range(FOX_HEADS):
        pair = slice((hd // 2) * LANES, (hd // 2 + 1) * LANES)
        blk = slice(hd * HEAD_PAD, (hd + 1) * HEAD_PAD)
        qb, kb = fq[:, pair], fk[:, pair]
        if hd % 2:
            qb, kb = pltpu.roll(qb, FOX_DIM, 1), pltpu.roll(kb, FOX_DIM, 1)
        c0 = (MLA_HEADS + hd) * HEAD_PAD
        qt_ref[0, c0:c0 + HEAD_PAD, :] = jnp.where(lane < FOX_DIM, qb, aug_q[:, blk]).T.astype(BF16)
        k_ref[0, :, c0:c0 + HEAD_PAD] = jnp.where(lane < FOX_DIM, kb, aug_k[:, blk]).astype(BF16)


def _resident(shape):
    return pl.BlockSpec(shape, lambda bi, si: (0,) * len(shape), pipeline_mode=pl.Buffered(1))


def _inproj_call(x, pos_f, norm_g, sc_mix, sh_mix, freq_row, bf_row, qg, kvg, weights, tm):
    b, s, d = x.shape
    tok = lambda w: pl.BlockSpec((1, tm, w), lambda bi, si: (bi, si, 0))
    vec = pl.BlockSpec((1, 1, d), lambda bi, si: (bi, 0, 0))
    return pl.pallas_call(
        functools.partial(_inproj_kernel, tm=tm),
        grid=(b, s // tm),
        in_specs=[tok(d), pl.BlockSpec((1, 1, tm), lambda bi, si: (bi, 0, si)),
                  _resident((1, d)), vec, vec, _resident((1, LANES)),
                  _resident((1, LANES)), _resident((1, MLA_Q_RANK)), _resident((1, MLA_KV_RANK))]
                 + [_resident(w.shape) for w in weights],
        out_specs=[pl.BlockSpec((1, N_HEADS * HEAD_PAD, tm), lambda bi, si: (bi, 0, si)),
                   tok(N_HEADS * HEAD_PAD),
                   pl.BlockSpec((1, N_HEADS * FOX_DIM, tm), lambda bi, si: (bi, 0, si)),
                   tok(2 * d)],
        out_shape=[jax.ShapeDtypeStruct((b, N_HEADS * HEAD_PAD, s), BF16),
                   jax.ShapeDtypeStruct((b, s, N_HEADS * HEAD_PAD), BF16),
                   jax.ShapeDtypeStruct((b, N_HEADS * FOX_DIM, s), BF16),
                   jax.ShapeDtypeStruct((b, s, 2 * d), BF16)],
        scratch_shapes=[pltpu.VMEM((1, LANES), F32)],
        compiler_params=pltpu.CompilerParams(dimension_semantics=("arbitrary", "arbitrary"),
                                             vmem_limit_bytes=VMEM_LIMIT),
        name="inproj",
    )(x, pos_f, norm_g, sc_mix, sh_mix, freq_row, bf_row, qg, kvg, *weights)


def _attn_kernel(qt_ref, k_ref, vt_ref, o_ref, st_ref, mx_ref, m_ref, acc_ref, *, tq, tk, nsub):
    qi = pl.program_id(2)
    assert tq == tk
    row = lax.broadcasted_iota(jnp.int32, (tk, tq), 0)
    col = lax.broadcasted_iota(jnp.int32, (tk, tq), 1)
    causal = row <= col
    ones_rows = jnp.ones((ONES_ROWS, tk), BF16)
    streams = [(hh, sub) for hh in range(2) for sub in range(nsub)]

    def scores(idx, j, slot, full):
        hh, sub = streams[idx]
        start = pl.multiple_of(j * tk, tk)
        qt = qt_ref[0, hh * HEAD_PAD:(hh + 1) * HEAD_PAD, sub * tq:(sub + 1) * tq]
        k = k_ref[0, pl.ds(start, tk), hh * HEAD_PAD:(hh + 1) * HEAD_PAD]
        st = jnp.dot(k, qt, preferred_element_type=F32)
        if full is False:
            st = jnp.where(causal, st, -jnp.inf)
        elif full is not True:
            st = jnp.where(row <= col + jnp.where(full, tk, 0), st, -jnp.inf)
        st_ref[slot, idx] = st
        mx_ref[slot, idx] = jnp.max(st, axis=0, keepdims=True)

    def update(idx, j, slot):
        hh, _ = streams[idx]
        start = pl.multiple_of(j * tk, tk)
        m = m_ref[idx]
        m_new = jnp.maximum(m, mx_ref[slot, idx])
        m_ref[idx] = m_new
        p = jnp.exp2(st_ref[slot, idx] - m_new).astype(BF16)
        vt = vt_ref[0, hh * MLA_V:(hh + 1) * MLA_V, pl.ds(start, tk)]
        vt_aug = jnp.concatenate([vt, ones_rows], axis=0)
        acc_ref[idx] = (jnp.exp2(m - m_new) * acc_ref[idx]
                        + jnp.dot(vt_aug, p, preferred_element_type=F32))

    def step(j, slot, diag_sub=None, next_is_full=None):
        first = 0 if diag_sub is None else diag_sub
        for idx, (_, sub) in enumerate(streams):
            if diag_sub is None:
                scores(idx, j + 1, 1 - slot, full=next_is_full if sub == 0 else True)
            elif sub > first:
                scores(idx, j + 1, 1 - slot, full=(sub != diag_sub + 1))
            if sub >= first:
                update(idx, j, slot)

    m_ref[...] = jnp.full(m_ref.shape, -jnp.inf, F32)
    acc_ref[...] = jnp.zeros(acc_ref.shape, F32)
    assert nsub % 2 == 0
    n_full = qi * nsub
    for idx, (_, sub) in enumerate(streams):
        scores(idx, 0, 0, full=(qi > 0) if sub == 0 else True)

    def body(i, carry):
        for u in range(nsub):
            step(nsub * i + u, u % 2, next_is_full=True if u < nsub - 1 else i < qi - 1)
        return carry

    lax.fori_loop(0, qi, body, 0)
    for r in range(nsub):
        step(n_full + r, r % 2, diag_sub=r)
        outs = []
        for hh in range(2):
            acc = acc_ref[streams.index((hh, r))]
            outs.append(acc[0:MLA_V, :] / acc[MLA_V:MLA_V + 1, :])
        o_ref[0, r * tq:(r + 1) * tq, :] = jnp.concatenate(outs, axis=0).T.astype(BF16)


def _attn_call(qt_all, k_all, vt_all, tq, nsub):
    b, s, _ = k_all.shape
    pairs = N_HEADS // 2
    bq = tq * nsub
    return pl.pallas_call(
        functools.partial(_attn_kernel, tq=tq, tk=tq, nsub=nsub),
        grid=(b, pairs, s // bq),
        in_specs=[pl.BlockSpec((1, 2 * HEAD_PAD, bq), lambda bi, pi, qi: (bi, pi, qi)),
                  pl.BlockSpec((1, s, 2 * HEAD_PAD), lambda bi, pi, qi: (bi, 0, pi)),
                  pl.BlockSpec((1, LANES, s), lambda bi, pi, qi: (bi, pi, 0))],
        out_specs=pl.BlockSpec((1, bq, LANES), lambda bi, pi, qi: (bi, qi, pi)),
        out_shape=jax.ShapeDtypeStruct((b, s, pairs * LANES), BF16),
        scratch_shapes=[pltpu.VMEM((2, 2 * nsub, tq, tq), F32),
                        pltpu.VMEM((2, 2 * nsub, 1, tq), F32),
                        pltpu.VMEM((2 * nsub, 1, tq), F32),
                        pltpu.VMEM((2 * nsub, MLA_V + ONES_ROWS, tq), F32)],
        compiler_params=pltpu.CompilerParams(
            dimension_semantics=("arbitrary", "arbitrary", "arbitrary"),
            vmem_limit_bytes=VMEM_LIMIT),
        name="attn",
    )(qt_all, k_all, vt_all)


def _mixer_residual(o_ref, g_ref, x_ref, gm_ref, wm_ref, wf_ref, wo_ref):
    half = MLA_HEADS * MLA_V
    o = o_ref[0]
    g = g_ref[0].astype(F32)
    y = (g[:, :D_MODEL] * jnp.dot(o[:, :half], wm_ref[...], preferred_element_type=F32)
         + g[:, D_MODEL:] * jnp.dot(o[:, half:], wf_ref[...], preferred_element_type=F32))
    mix = jnp.dot(y.astype(BF16), wo_ref[...], preferred_element_type=F32)
    return x_ref[0] + gm_ref[0] * mix


def _interleave_rows(x, perm_ref):
    tm, d = x.shape
    group = tm // 8
    for ls in range(d // LANES):
        for r in range(8):
            perm_ref[ls, r * PERM_PITCH:r * PERM_PITCH + group, :] = (
                x[r * group:(r + 1) * group, ls * LANES:(ls + 1) * LANES])
    return jnp.concatenate(
        [jnp.concatenate([perm_ref[ls, pl.ds(i, 8, stride=PERM_PITCH), :] for i in range(group)],
                         axis=0) for ls in range(d // LANES)], axis=1)


def _tail_kernel(o_ref, g_ref, x_ref, gm_ref, wm_ref, wf_ref, wo_ref,
                 ng_ref, sc_ref, sh_ref, gf_ref, cw_ref, cb_ref, fg_ref, wup_ref, wdn_ref,
                 out_ref, tail_ref, act_ref, perm_ref, *, chunk):
    s_idx = pl.program_id(1)

    @pl.when(s_idx == 0)
    def _():
        tail_ref[...] = jnp.zeros_like(tail_ref)

    x1 = _interleave_rows(
        _mixer_residual(o_ref, g_ref, x_ref, gm_ref, wm_ref, wf_ref, wo_ref), perm_ref)
    tm = x1.shape[0]
    h = _modulated_norm(x1, ng_ref, sc_ref, sh_ref)
    first_row = lax.broadcasted_iota(jnp.int32, (8, chunk), 0) < 1
    assert cw_ref.shape[0] == CONV_WIDTH == 3

    def conv(c0):
        u = jnp.dot(h, wup_ref[:, c0:c0 + chunk], preferred_element_type=F32)
        prev = tail_ref[:, c0:c0 + chunk]
        tail_ref[:, c0:c0 + chunk] = u[tm - 16:tm, :]
        wrap = [jnp.where(first_row, pltpu.roll(prev[8 * i:8 * i + 8, :], 1, 0),
                          pltpu.roll(u[tm - 16 + 8 * i:tm - 8 + 8 * i, :], 1, 0)) for i in range(2)]
        back1 = jnp.concatenate([wrap[1], u[0:tm - 8, :]], axis=0)
        back2 = jnp.concatenate([wrap[0], wrap[1], u[0:tm - 16, :]], axis=0)
        w = cw_ref[:, c0:c0 + chunk]
        return cb_ref[:, c0:c0 + chunk] + w[0:1] * back2 + w[1:2] * back1 + w[2:3] * u

    for c0 in range(0, D_FF, chunk):
        act_ref[:, c0:c0 + chunk] = (_silu(conv(c0)) * conv(D_FF + c0)).astype(BF16)

    group = tm // 8
    for n0 in range(0, D_MODEL, 2 * LANES):
        cols = slice(n0, n0 + 2 * LANES)
        y = jnp.dot(act_ref[...], wdn_ref[:, cols], preferred_element_type=F32)
        x2 = x1[:, cols] + gf_ref[0][:, cols] * y
        for half in range(2):
            ls = n0 // LANES + half
            for i in range(group):
                perm_ref[ls, pl.ds(i, 8, stride=PERM_PITCH), :] = (
                    x2[8 * i:8 * i + 8, half * LANES:(half + 1) * LANES])
    x2 = jnp.concatenate(
        [jnp.concatenate([perm_ref[ls, r * PERM_PITCH:r * PERM_PITCH + group, :]
                          for r in range(8)], axis=0) for ls in range(D_MODEL // LANES)], axis=1)
    out_ref[0] = _rms_scale(x2) * fg_ref[...]


def _tail_call(o_all, g_all, x, g_m, wm, wf, wo, norm_g, sc_ffn, sh_ffn, g_f, conv_w, conv_b,
               final_g, wup, wdn, tm, chunk):
    b, s, d = x.shape
    const = _resident
    tok = lambda w: pl.BlockSpec((1, tm, w), lambda bi, si: (bi, si, 0))
    vec = pl.BlockSpec((1, 1, d), lambda bi, si: (bi, 0, 0))
    return pl.pallas_call(
        functools.partial(_tail_kernel, chunk=chunk),
        grid=(b, s // tm),
        in_specs=[tok(o_all.shape[-1]), tok(g_all.shape[-1]), tok(d), vec,
                  const(wm.shape), const(wf.shape), const(wo.shape),
                  const((1, d)), vec, vec, vec, const(conv_w.shape), const(conv_b.shape),
                  const(final_g.shape), const(wup.shape), const(wdn.shape)],
        out_specs=tok(d),
        out_shape=jax.ShapeDtypeStruct((b, s, d), F32),
        scratch_shapes=[pltpu.VMEM((16, 2 * D_FF), F32), pltpu.VMEM((tm, D_FF), BF16),
                        pltpu.VMEM((d // LANES, 8 * PERM_PITCH, LANES), F32)],
        compiler_params=pltpu.CompilerParams(dimension_semantics=("arbitrary", "arbitrary"),
                                             vmem_limit_bytes=VMEM_LIMIT),
        name="tail",
    )(o_all, g_all, x, g_m, wm, wf, wo, norm_g, sc_ffn, sh_ffn, g_f, conv_w, conv_b, final_g,
      wup, wdn)


def _pad_heads(w, heads, width):
    k = w.shape[0]
    w = w.reshape(k, heads, width)
    return jnp.pad(w, ((0, 0), (0, 0), (0, HEAD_PAD - width))).reshape(k, heads * HEAD_PAD)


def _prep_w_in(w_in):
    d = w_in.shape[0]
    splits = (MLA_Q_RANK, MLA_KV_RANK, MLA_ROPE, FOX_HEADS * FOX_DIM, FOX_HEADS * FOX_DIM,
              FOX_HEADS * FOX_DIM, FOX_HEADS, D_MODEL, D_MODEL)
    offs = [0]
    for n in splits:
        offs.append(offs[-1] + n)
    w_in = w_in.astype(BF16)
    piece = lambda i, j=None: w_in[:, offs[i]:offs[i + 1 if j is None else j]]
    misc = jnp.concatenate([jnp.zeros((d, MLA_NOPE), BF16), piece(2), piece(6),
                            jnp.zeros((d, LANES - FLOGIT_LO - FOX_HEADS), BF16)], axis=1)
    return [jnp.concatenate([piece(0, 2), misc], axis=1), piece(3, 6), piece(7, 9)]


def _bias_placement():
    part, head = np.meshgrid(np.arange(3), np.arange(FOX_HEADS), indexing="ij")
    rows = (FLOGIT_LO + part * FOX_HEADS + head).reshape(-1)
    q_cols = (head * HEAD_PAD + AUG_LO + part).reshape(-1)
    k_cols = (head * HEAD_PAD + AUG_LO + 3 + part).reshape(-1)
    width = FOX_HEADS * HEAD_PAD
    sq, sk = np.zeros((LANES, width), np.float32), np.zeros((LANES, width), np.float32)
    qc, kc = np.zeros((1, width), np.float32), np.zeros((1, width), np.float32)
    sq[rows, q_cols] = 1.0
    sk[rows, k_cols] = -1.0
    qc[0, k_cols] = 1.0
    kc[0, q_cols] = 1.0
    return jnp.asarray(sq, BF16), jnp.asarray(sk, BF16), jnp.asarray(qc), jnp.asarray(kc)


def kernel(x, c, positions, w_ada, b_ada, norm_mix_g, w_in, q_norm_g, w_uq, kv_norm_g, w_ukv,
           b_forget, w_o_mla, w_o_fox, w_out, norm_ffn_g, w_up, conv_w, conv_b, w_down,
           norm_final_g):
    assert w_ada.shape[0] == 1, "single-layer block"
    b, s, d = x.shape
    ada = _ada_call(c, w_ada[0], b_ada[0])
    sh_m, sc_m, g_m, sh_f, sc_f, g_f = (ada[:, i * d:(i + 1) * d].reshape(b, 1, d)
                                        for i in range(N_ADA))

    inv_freq = ROPE_THETA ** (-jnp.arange(0, MLA_ROPE, 2, dtype=F32) / MLA_ROPE)
    freq_row = jnp.concatenate([jnp.zeros((MLA_NOPE,), F32), inv_freq, inv_freq,
                                jnp.zeros((LANES - FLOGIT_LO,), F32)]).reshape(1, LANES)
    bf_row = jnp.concatenate([jnp.zeros((FLOGIT_LO,), F32), b_forget[0].astype(F32),
                              jnp.zeros((LANES - FLOGIT_LO - FOX_HEADS,), F32)]).reshape(1, LANES)
    pos_f = positions.astype(F32).reshape(b, 1, s)

    wuq = _pad_heads(w_uq[0], MLA_HEADS, MLA_NOPE + MLA_ROPE).astype(BF16)
    wukv = w_ukv[0].reshape(MLA_KV_RANK, MLA_HEADS, MLA_NOPE + MLA_V)
    wukv = jnp.concatenate([wukv[:, :, :MLA_NOPE].reshape(MLA_KV_RANK, -1),
                            wukv[:, :, MLA_NOPE:].reshape(MLA_KV_RANK, -1)], axis=1).astype(BF16)

    qt_all, k_all, vt_all, g_all = _inproj_call(
        x, pos_f, norm_mix_g[0].reshape(1, d), sc_m, sh_m, freq_row, bf_row,
        q_norm_g[0].reshape(1, -1), kv_norm_g[0].reshape(1, -1),
        _prep_w_in(w_in[0]) + [wuq, wukv] + list(_bias_placement()), tm=INPROJ_ROWS)
    o_all = _attn_call(qt_all, k_all, vt_all, tq=ATTN_TILE, nsub=ATTN_SUBBLOCKS)
    return _tail_call(o_all, g_all, x, g_m, w_o_mla[0].astype(BF16), w_o_fox[0].astype(BF16),
                      w_out[0].astype(BF16), norm_ffn_g[0].reshape(1, d), sc_f, sh_f, g_f, conv_w[0],
                      conv_b[0].reshape(1, -1), norm_final_g.reshape(1, -1),
                      w_up[0].astype(BF16), w_down[0].astype(BF16), tm=TAIL_ROWS,
                      chunk=FFN_CHUNK)
```

```python
import functools
import math

import jax
import jax.numpy as jnp
import numpy as np
from jax import lax
from jax.experimental import pallas as pl
from jax.experimental.pallas import tpu as pltpu

F32 = jnp.float32
BF16 = jnp.bfloat16

D_MODEL = 1024
MLA_HEADS = 8
MLA_Q_RANK = 384
MLA_KV_RANK = 256
MLA_NOPE = 64
MLA_ROPE = 32
MLA_V = 64
MLA_SCALE = 1.0 / math.sqrt(MLA_NOPE + MLA_ROPE)
ROPE_THETA = 10000.0
FOX_HEADS = 8
FOX_DIM = 64
FOX_SCALE = 1.0 / math.sqrt(FOX_DIM)
LOG2E = math.log2(math.e)
D_FF = 2816
CONV_WIDTH = 3
EPS = 1e-6
N_ADA = 6

LANES = 128
HEAD_PAD = LANES
N_HEADS = MLA_HEADS + FOX_HEADS
ROPE_HALF = MLA_ROPE // 2
ROPE_LO = MLA_NOPE
ROPE_HI = MLA_NOPE + ROPE_HALF
FLOGIT_LO = MLA_NOPE + MLA_ROPE
AUG_LO = FOX_DIM
ONES_ROWS = 16

VMEM_LIMIT = 56 * 1024 * 1024

ADA_COLS = 1536
INPROJ_ROWS = 512
TAIL_ROWS = 512
FFN_CHUNK = 256
ATTN_TILE = 256
ATTN_SUBBLOCKS = 8
PERM_PITCH = TAIL_ROWS // 8 + 8


def _silu(x):
    return x * (1.0 / (1.0 + jnp.exp(-x)))


def _sigmoid(x):
    return 1.0 / (1.0 + jnp.exp(-x))


def _rms_scale(x):
    return x * lax.rsqrt(jnp.mean(x * x, axis=-1, keepdims=True) + EPS)


def _ada_kernel(ct_ref, w_ref, b_ref, o_ref, *, batch):
    act = _silu(ct_ref[...])
    w = w_ref[...]
    for b in range(batch):
        o_ref[b:b + 1, :] = jnp.sum(w * act[:, b:b + 1], axis=0, keepdims=True) + b_ref[...]


def _ada_call(c, w_ada, b_ada):
    batch, d = c.shape
    n = w_ada.shape[1]
    tn = ADA_COLS
    return pl.pallas_call(
        functools.partial(_ada_kernel, batch=batch),
        grid=(n // tn,),
        in_specs=[pl.BlockSpec((d, batch), lambda j: (0, 0)),
                  pl.BlockSpec((d, tn), lambda j: (0, j)),
                  pl.BlockSpec((1, tn), lambda j: (0, j))],
        out_specs=pl.BlockSpec((batch, tn), lambda j: (0, j)),
        out_shape=jax.ShapeDtypeStruct((batch, n), F32),
        compiler_params=pltpu.CompilerParams(dimension_semantics=("arbitrary",),
                                             vmem_limit_bytes=VMEM_LIMIT),
        name="ada",
    )(c.T, w_ada, b_ada.reshape(1, n))


def _rope_tables(pos_col, freq_row):
    ang = pos_col * freq_row
    lane = lax.broadcasted_iota(jnp.int32, ang.shape, 1)
    cos = jnp.cos(ang)
    sin = jnp.sin(ang)
    s_lo = jnp.where((lane >= ROPE_LO) & (lane < ROPE_HI), -sin, 0.0)
    s_hi = jnp.where((lane >= ROPE_HI) & (lane < FLOGIT_LO), sin, 0.0)
    return cos, s_lo, s_hi


def _rope_block(t, cos, s_lo, s_hi):
    return (t * cos + pltpu.roll(t, LANES - ROPE_HALF, 1) * s_lo
            + pltpu.roll(t, ROPE_HALF, 1) * s_hi)


def _split3(x):
    hi = x.astype(BF16)
    r1 = x - hi.astype(F32)
    mid = r1.astype(BF16)
    lo = (r1 - mid.astype(F32)).astype(BF16)
    return hi, mid, lo


def _pack3(x):
    hi, mid, lo = (p.astype(F32) for p in _split3(x))
    return (hi + pltpu.roll(mid, FOX_HEADS, 1) + pltpu.roll(lo, 2 * FOX_HEADS, 1)).astype(BF16)


def _modulated_norm(x, gain_ref, sc_ref, sh_ref):
    return (_rms_scale(x) * (gain_ref[...] * (1.0 + sc_ref[0])) + sh_ref[0]).astype(BF16)


def _inproj_kernel(x_ref, pos_ref, ng_ref, sc_ref, sh_ref, freq_ref, bf_ref, qg_ref, kvg_ref,
                   wlat_ref, wfox_ref, wgate_ref,
                   wuq_ref, wukv_ref, sq_ref, sk_ref, qc_ref, kc_ref,
                   qt_ref, k_ref, vt_ref, g_ref, carry_ref, *, tm):
    s_idx = pl.program_id(1)

    @pl.when(s_idx == 0)
    def _():
        carry_ref[...] = jnp.zeros_like(carry_ref)

    h = _modulated_norm(x_ref[0], ng_ref, sc_ref, sh_ref)

    def proj(w_ref, lo=None, hi=None):
        w = w_ref[...] if lo is None else w_ref[:, lo:hi]
        return jnp.dot(h, w, preferred_element_type=F32)

    lane = lax.broadcasted_iota(jnp.int32, (tm, LANES), 1)
    pos_col = jnp.broadcast_to(pos_ref[0], (8, tm)).T[:, 0:1]
    cos, s_lo, s_hi = _rope_tables(pos_col, freq_ref[...])

    lat = proj(wlat_ref)
    lat_q = lat[:, 0:MLA_Q_RANK]
    lat_kv = lat[:, MLA_Q_RANK:MLA_Q_RANK + MLA_KV_RANK]
    misc = lat[:, MLA_Q_RANK + MLA_KV_RANK:]
    g_ref[0, :, 0:D_MODEL] = _sigmoid(proj(wgate_ref, 0, D_MODEL)).astype(BF16)

    cqn = (_rms_scale(lat_q) * qg_ref[...]).astype(BF16)
    ckvn = (_rms_scale(lat_kv) * kvg_ref[...]).astype(BF16)
    qm = jnp.dot(cqn, wuq_ref[...], preferred_element_type=F32)
    kvm = jnp.dot(ckvn, wukv_ref[...], preferred_element_type=F32)
    kn, vm = kvm[:, 0:MLA_HEADS * MLA_NOPE], kvm[:, MLA_HEADS * MLA_NOPE:]
    fv = proj(wfox_ref, 2 * FOX_HEADS * FOX_DIM, 3 * FOX_HEADS * FOX_DIM)
    for hd in range(MLA_HEADS):
        blk = qm[:, hd * HEAD_PAD:(hd + 1) * HEAD_PAD]
        qt_ref[0, hd * HEAD_PAD:(hd + 1) * HEAD_PAD, :] = (
            _rope_block(blk, cos, s_lo, s_hi) * (MLA_SCALE * LOG2E)).T.astype(BF16)

    krope = _rope_block(jnp.where(lane < FLOGIT_LO, misc, 0.0), cos, s_lo, s_hi)
    for hd in range(MLA_HEADS):
        kb = kn[:, (hd // 2) * LANES:(hd // 2 + 1) * LANES]
        if hd % 2:
            kb = pltpu.roll(kb, MLA_NOPE, 1)
        k_ref[0, :, hd * HEAD_PAD:(hd + 1) * HEAD_PAD] = jnp.where(
            lane < MLA_NOPE, kb, krope).astype(BF16)
    vt_ref[0, 0:MLA_HEADS * MLA_V, :] = vm.T.astype(BF16)
    vt_ref[0, MLA_HEADS * MLA_V:, :] = fv.T.astype(BF16)

    g_ref[0, :, D_MODEL:] = _sigmoid(proj(wgate_ref, D_MODEL, 2 * D_MODEL)).astype(BF16)

    z = misc + bf_ref[...]
    logf = jnp.minimum(z, 0.0) - jnp.log1p(jnp.exp(-jnp.abs(z)))
    logf = jnp.where((lane >= FLOGIT_LO) & (lane < FLOGIT_LO + FOX_HEADS), logf, 0.0)
    row = lax.broadcasted_iota(jnp.int32, (tm, tm), 0)
    col = lax.broadcasted_iota(jnp.int32, (tm, tm), 1)
    tri = jnp.where(col <= row, 1.0, 0.0).astype(BF16)
    parts = jnp.dot(tri, _pack3(logf), preferred_element_type=F32)
    total = (parts + pltpu.roll(parts, LANES - FOX_HEADS, 1)
             + pltpu.roll(parts, LANES - 2 * FOX_HEADS, 1))
    fcum = carry_ref[...] + jnp.where(
        (lane >= FLOGIT_LO) & (lane < FLOGIT_LO + FOX_HEADS), total, 0.0)
    carry_ref[...] = fcum[tm - 1:tm, :]
    fq = proj(wfox_ref, 0, FOX_HEADS * FOX_DIM) * (FOX_SCALE * LOG2E)
    fk = proj(wfox_ref, FOX_HEADS * FOX_DIM, 2 * FOX_HEADS * FOX_DIM)

    bias3 = _pack3(fcum * LOG2E)
    aug_q = jnp.dot(bias3, sq_ref[...], preferred_element_type=F32) + qc_ref[...]
    aug_k = jnp.dot(bias3, sk_ref[...], preferred_element_type=F32) + kc_ref[...]
    for hd in range(FOX_HEADS):
        pair = slice((hd // 2) * LANES, (hd // 2 + 1) * LANES)
        blk = slice(hd * HEAD_PAD, (hd + 1) * HEAD_PAD)
        qb, kb = fq[:, pair], fk[:, pair]
        if hd % 2:
            qb, kb = pltpu.roll(qb, FOX_DIM, 1), pltpu.roll(kb, FOX_DIM, 1)
        c0 = (MLA_HEADS + hd) * HEAD_PAD
        qt_ref[0, c0:c0 + HEAD_PAD, :] = jnp.where(lane < FOX_DIM, qb, aug_q[:, blk]).T.astype(BF16)
        k_ref[0, :, c0:c0 + HEAD_PAD] = jnp.where(lane < FOX_DIM, kb, aug_k[:, blk]).astype(BF16)


def _resident(shape):
    return pl.BlockSpec(shape, lambda bi, si: (0,) * len(shape), pipeline_mode=pl.Buffered(1))


def _inproj_call(x, pos_f, norm_g, sc_mix, sh_mix, freq_row, bf_row, qg, kvg, weights, tm):
    b, s, d = x.shape
    tok = lambda w: pl.BlockSpec((1, tm, w), lambda bi, si: (bi, si, 0))
    vec = pl.BlockSpec((1, 1, d), lambda bi, si: (bi, 0, 0))
    return pl.pallas_call(
        functools.partial(_inproj_kernel, tm=tm),
        grid=(b, s // tm),
        in_specs=[tok(d), pl.BlockSpec((1, 1, tm), lambda bi, si: (bi, 0, si)),
                  _resident((1, d)), vec, vec, _resident((1, LANES)),
                  _resident((1, LANES)), _resident((1, MLA_Q_RANK)), _resident((1, MLA_KV_RANK))]
                 + [_resident(w.shape) for w in weights],
        out_specs=[pl.BlockSpec((1, N_HEADS * HEAD_PAD, tm), lambda bi, si: (bi, 0, si)),
                   tok(N_HEADS * HEAD_PAD),
                   pl.BlockSpec((1, N_HEADS * FOX_DIM, tm), lambda bi, si: (bi, 0, si)),
                   tok(2 * d)],
        out_shape=[jax.ShapeDtypeStruct((b, N_HEADS * HEAD_PAD, s), BF16),
                   jax.ShapeDtypeStruct((b, s, N_HEADS * HEAD_PAD), BF16),
                   jax.ShapeDtypeStruct((b, N_HEADS * FOX_DIM, s), BF16),
                   jax.ShapeDtypeStruct((b, s, 2 * d), BF16)],
        scratch_shapes=[pltpu.VMEM((1, LANES), F32)],
        compiler_params=pltpu.CompilerParams(dimension_semantics=("arbitrary", "arbitrary"),
                                             vmem_limit_bytes=VMEM_LIMIT),
        name="inproj",
    )(x, pos_f, norm_g, sc_mix, sh_mix, freq_row, bf_row, qg, kvg, *weights)


def _attn_kernel(qt_ref, k_ref, vt_ref, o_ref, st_ref, mx_ref, m_ref, acc_ref, *, tq, tk, nsub):
    assert tq == tk
    bq = nsub * tq
    n_blocks = k_ref.shape[1] // bq
    row = lax.broadcasted_iota(jnp.int32, (tk, tq), 0)
    col = lax.broadcasted_iota(jnp.int32, (tk, tq), 1)
    causal = row <= col
    ones_rows = jnp.ones((ONES_ROWS, tk), BF16)
    streams = [(hh, sub) for hh in range(2) for sub in range(nsub)]

    def scores(idx, qb, j, slot, full):
        hh, sub = streams[idx]
        start = pl.multiple_of(j * tk, tk)
        q0 = pl.multiple_of(qb * bq + sub * tq, tq)
        qt = qt_ref[0, hh * HEAD_PAD:(hh + 1) * HEAD_PAD, pl.ds(q0, tq)]
        k = k_ref[0, pl.ds(start, tk), hh * HEAD_PAD:(hh + 1) * HEAD_PAD]
        st = jnp.dot(k, qt, preferred_element_type=F32)
        if full is False:
            st = jnp.where(causal, st, -jnp.inf)
        elif full is not True:
            st = jnp.where(row <= col + jnp.where(full, tk, 0), st, -jnp.inf)
        st_ref[slot, idx] = st
        mx_ref[slot, idx] = jnp.max(st, axis=0, keepdims=True)

    def update(idx, j, slot):
        hh, _ = streams[idx]
        start = pl.multiple_of(j * tk, tk)
        m = m_ref[idx]
        m_new = jnp.maximum(m, mx_ref[slot, idx])
        m_ref[idx] = m_new
        p = jnp.exp2(st_ref[slot, idx] - m_new).astype(BF16)
        vt = vt_ref[0, hh * MLA_V:(hh + 1) * MLA_V, pl.ds(start, tk)]
        vt_aug = jnp.concatenate([vt, ones_rows], axis=0)
        acc_ref[idx] = (jnp.exp2(m - m_new) * acc_ref[idx]
                        + jnp.dot(vt_aug, p, preferred_element_type=F32))

    def reset(idx):
        m_ref[idx] = jnp.full(m_ref.shape[1:], -jnp.inf, F32)
        acc_ref[idx] = jnp.zeros(acc_ref.shape[1:], F32)

    def step(qb, j, slot, diag_sub=None, next_is_full=None):
        first = 0 if diag_sub is None else diag_sub
        for idx, (_, sub) in enumerate(streams):
            if diag_sub is None:
                scores(idx, qb, j + 1, 1 - slot, full=next_is_full if sub == 0 else True)
            elif sub > first:
                scores(idx, qb, j + 1, 1 - slot, full=(sub != diag_sub + 1))
            if sub >= first:
                update(idx, j, slot)

    assert nsub % 2 == 0
    for idx, (_, sub) in enumerate(streams):
        reset(idx)
        scores(idx, 0, 0, 0, full=(sub != 0))

    def query_block(qb, carry):
        n_full = qb * nsub

        def body(i, c):
            for u in range(nsub):
                step(qb, nsub * i + u, u % 2, next_is_full=True if u < nsub - 1 else i < qb - 1)
            return c

        lax.fori_loop(0, qb, body, 0)
        nxt = jnp.minimum(qb + 1, n_blocks - 1)
        for r in range(nsub):
            step(qb, n_full + r, r % 2, diag_sub=r)
            outs = []
            for hh in range(2):
                acc = acc_ref[streams.index((hh, r))]
                outs.append(acc[0:MLA_V, :] / acc[MLA_V:MLA_V + 1, :])
            o0 = pl.multiple_of(qb * bq + r * tq, tq)
            o_ref[0, pl.ds(o0, tq), :] = jnp.concatenate(outs, axis=0).T.astype(BF16)
            for hh in range(2):
                idx = streams.index((hh, r))
                reset(idx)
                scores(idx, nxt, 0, 0, full=True)
        return carry

    lax.fori_loop(0, n_blocks, query_block, 0)


def _attn_call(qt_all, k_all, vt_all, tq, nsub):
    b, s, _ = k_all.shape
    pairs = N_HEADS // 2
    return pl.pallas_call(
        functools.partial(_attn_kernel, tq=tq, tk=tq, nsub=nsub),
        grid=(b, pairs),
        in_specs=[pl.BlockSpec((1, 2 * HEAD_PAD, s), lambda bi, pi: (bi, pi, 0)),
                  pl.BlockSpec((1, s, 2 * HEAD_PAD), lambda bi, pi: (bi, 0, pi)),
                  pl.BlockSpec((1, LANES, s), lambda bi, pi: (bi, pi, 0))],
        out_specs=pl.BlockSpec((1, s, LANES), lambda bi, pi: (bi, 0, pi)),
        out_shape=jax.ShapeDtypeStruct((b, s, pairs * LANES), BF16),
        scratch_shapes=[pltpu.VMEM((2, 2 * nsub, tq, tq), F32),
                        pltpu.VMEM((2, 2 * nsub, 1, tq), F32),
                        pltpu.VMEM((2 * nsub, 1, tq), F32),
                        pltpu.VMEM((2 * nsub, MLA_V + ONES_ROWS, tq), F32)],
        compiler_params=pltpu.CompilerParams(dimension_semantics=("arbitrary", "arbitrary"),
                                             vmem_limit_bytes=VMEM_LIMIT),
        name="attn",
    )(qt_all, k_all, vt_all)


def _mixer_residual(o_ref, g_ref, x_ref, gm_ref, wm_ref, wf_ref, wo_ref):
    half = MLA_HEADS * MLA_V
    o = o_ref[0]
    g = g_ref[0].astype(F32)
    y = (g[:, :D_MODEL] * jnp.dot(o[:, :half], wm_ref[...], preferred_element_type=F32)
         + g[:, D_MODEL:] * jnp.dot(o[:, half:], wf_ref[...], preferred_element_type=F32))
    mix = jnp.dot(y.astype(BF16), wo_ref[...], preferred_element_type=F32)
    return x_ref[0] + gm_ref[0] * mix


def _interleave_rows(x, perm_ref):
    tm, d = x.shape
    group = tm // 8
    for ls in range(d // LANES):
        for r in range(8):
            perm_ref[ls, r * PERM_PITCH:r * PERM_PITCH + group, :] = (
                x[r * group:(r + 1) * group, ls * LANES:(ls + 1) * LANES])
    return jnp.concatenate(
        [jnp.concatenate([perm_ref[ls, pl.ds(i, 8, stride=PERM_PITCH), :] for i in range(group)],
                         axis=0) for ls in range(d // LANES)], axis=1)


def _tail_kernel(o_ref, g_ref, x_ref, gm_ref, wm_ref, wf_ref, wo_ref,
                 ng_ref, sc_ref, sh_ref, gf_ref, cw_ref, cb_ref, fg_ref, wup_ref, wdn_ref,
                 out_ref, tail_ref, act_ref, perm_ref, *, chunk):
    s_idx = pl.program_id(1)

    @pl.when(s_idx == 0)
    def _():
        tail_ref[...] = jnp.zeros_like(tail_ref)

    x1 = _interleave_rows(
        _mixer_residual(o_ref, g_ref, x_ref, gm_ref, wm_ref, wf_ref, wo_ref), perm_ref)
    tm = x1.shape[0]
    h = _modulated_norm(x1, ng_ref, sc_ref, sh_ref)
    first_row = lax.broadcasted_iota(jnp.int32, (8, chunk), 0) < 1
    assert cw_ref.shape[0] == CONV_WIDTH == 3

    def conv(c0):
        u = jnp.dot(h, wup_ref[:, c0:c0 + chunk], preferred_element_type=F32)
        prev = tail_ref[:, c0:c0 + chunk]
        tail_ref[:, c0:c0 + chunk] = u[tm - 16:tm, :]
        wrap = [jnp.where(first_row, pltpu.roll(prev[8 * i:8 * i + 8, :], 1, 0),
                          pltpu.roll(u[tm - 16 + 8 * i:tm - 8 + 8 * i, :], 1, 0)) for i in range(2)]
        back1 = jnp.concatenate([wrap[1], u[0:tm - 8, :]], axis=0)
        back2 = jnp.concatenate([wrap[0], wrap[1], u[0:tm - 16, :]], axis=0)
        w = cw_ref[:, c0:c0 + chunk]
        return cb_ref[:, c0:c0 + chunk] + w[0:1] * back2 + w[1:2] * back1 + w[2:3] * u

    for c0 in range(0, D_FF, chunk):
        act_ref[:, c0:c0 + chunk] = (_silu(conv(c0)) * conv(D_FF + c0)).astype(BF16)

    group = tm // 8
    for n0 in range(0, D_MODEL, 2 * LANES):
        cols = slice(n0, n0 + 2 * LANES)
        y = jnp.dot(act_ref[...], wdn_ref[:, cols], preferred_element_type=F32)
        x2 = x1[:, cols] + gf_ref[0][:, cols] * y
        for half in range(2):
            ls = n0 // LANES + half
            for i in range(group):
                perm_ref[ls, pl.ds(i, 8, stride=PERM_PITCH), :] = (
                    x2[8 * i:8 * i + 8, half * LANES:(half + 1) * LANES])
    x2 = jnp.concatenate(
        [jnp.concatenate([perm_ref[ls, r * PERM_PITCH:r * PERM_PITCH + group, :]
                          for r in range(8)], axis=0) for ls in range(D_MODEL // LANES)], axis=1)
    out_ref[0] = _rms_scale(x2) * fg_ref[...]


def _tail_call(o_all, g_all, x, g_m, wm, wf, wo, norm_g, sc_ffn, sh_ffn, g_f, conv_w, conv_b,
               final_g, wup, wdn, tm, chunk):
    b, s, d = x.shape
    const = _resident
    tok = lambda w: pl.BlockSpec((1, tm, w), lambda bi, si: (bi, si, 0))
    vec = pl.BlockSpec((1, 1, d), lambda bi, si: (bi, 0, 0))
    return pl.pallas_call(
        functools.partial(_tail_kernel, chunk=chunk),
        grid=(b, s // tm),
        in_specs=[tok(o_all.shape[-1]), tok(g_all.shape[-1]), tok(d), vec,
                  const(wm.shape), const(wf.shape), const(wo.shape),
                  const((1, d)), vec, vec, vec, const(conv_w.shape), const(conv_b.shape),
                  const(final_g.shape), const(wup.shape), const(wdn.shape)],
        out_specs=tok(d),
        out_shape=jax.ShapeDtypeStruct((b, s, d), F32),
        scratch_shapes=[pltpu.VMEM((16, 2 * D_FF), F32), pltpu.VMEM((tm, D_FF), BF16),
                        pltpu.VMEM((d // LANES, 8 * PERM_PITCH, LANES), F32)],
        compiler_params=pltpu.CompilerParams(dimension_semantics=("arbitrary", "arbitrary"),
                                             vmem_limit_bytes=VMEM_LIMIT),
        name="tail",
    )(o_all, g_all, x, g_m, wm, wf, wo, norm_g, sc_ffn, sh_ffn, g_f, conv_w, conv_b, final_g,
      wup, wdn)


def _pad_heads(w, heads, width):
    k = w.shape[0]
    w = w.reshape(k, heads, width)
    return jnp.pad(w, ((0, 0), (0, 0), (0, HEAD_PAD - width))).reshape(k, heads * HEAD_PAD)


def _prep_w_in(w_in):
    d = w_in.shape[0]
    splits = (MLA_Q_RANK, MLA_KV_RANK, MLA_ROPE, FOX_HEADS * FOX_DIM, FOX_HEADS * FOX_DIM,
              FOX_HEADS * FOX_DIM, FOX_HEADS, D_MODEL, D_MODEL)
    offs = [0]
    for n in splits:
        offs.append(offs[-1] + n)
    w_in = w_in.astype(BF16)
    piece = lambda i, j=None: w_in[:, offs[i]:offs[i + 1 if j is None else j]]
    misc = jnp.concatenate([jnp.zeros((d, MLA_NOPE), BF16), piece(2), piece(6),
                            jnp.zeros((d, LANES - FLOGIT_LO - FOX_HEADS), BF16)], axis=1)
    return [jnp.concatenate([piece(0, 2), misc], axis=1), piece(3, 6), piece(7, 9)]


def _bias_placement():
    part, head = np.meshgrid(np.arange(3), np.arange(FOX_HEADS), indexing="ij")
    rows = (FLOGIT_LO + part * FOX_HEADS + head).reshape(-1)
    q_cols = (head * HEAD_PAD + AUG_LO + part).reshape(-1)
    k_cols = (head * HEAD_PAD + AUG_LO + 3 + part).reshape(-1)
    width = FOX_HEADS * HEAD_PAD
    sq, sk = np.zeros((LANES, width), np.float32), np.zeros((LANES, width), np.float32)
    qc, kc = np.zeros((1, width), np.float32), np.zeros((1, width), np.float32)
    sq[rows, q_cols] = 1.0
    sk[rows, k_cols] = -1.0
    qc[0, k_cols] = 1.0
    kc[0, q_cols] = 1.0
    return jnp.asarray(sq, BF16), jnp.asarray(sk, BF16), jnp.asarray(qc), jnp.asarray(kc)


def kernel(x, c, positions, w_ada, b_ada, norm_mix_g, w_in, q_norm_g, w_uq, kv_norm_g, w_ukv,
           b_forget, w_o_mla, w_o_fox, w_out, norm_ffn_g, w_up, conv_w, conv_b, w_down,
           norm_final_g):
    assert w_ada.shape[0] == 1, "single-layer block"
    b, s, d = x.shape
    ada = _ada_call(c, w_ada[0], b_ada[0])
    sh_m, sc_m, g_m, sh_f, sc_f, g_f = (ada[:, i * d:(i + 1) * d].reshape(b, 1, d)
                                        for i in range(N_ADA))

    inv_freq = ROPE_THETA ** (-jnp.arange(0, MLA_ROPE, 2, dtype=F32) / MLA_ROPE)
    freq_row = jnp.concatenate([jnp.zeros((MLA_NOPE,), F32), inv_freq, inv_freq,
                                jnp.zeros((LANES - FLOGIT_LO,), F32)]).reshape(1, LANES)
    bf_row = jnp.concatenate([jnp.zeros((FLOGIT_LO,), F32), b_forget[0].astype(F32),
                              jnp.zeros((LANES - FLOGIT_LO - FOX_HEADS,), F32)]).reshape(1, LANES)
    pos_f = positions.astype(F32).reshape(b, 1, s)

    wuq = _pad_heads(w_uq[0], MLA_HEADS, MLA_NOPE + MLA_ROPE).astype(BF16)
    wukv = w_ukv[0].reshape(MLA_KV_RANK, MLA_HEADS, MLA_NOPE + MLA_V)
    wukv = jnp.concatenate([wukv[:, :, :MLA_NOPE].reshape(MLA_KV_RANK, -1),
                            wukv[:, :, MLA_NOPE:].reshape(MLA_KV_RANK, -1)], axis=1).astype(BF16)

    qt_all, k_all, vt_all, g_all = _inproj_call(
        x, pos_f, norm_mix_g[0].reshape(1, d), sc_m, sh_m, freq_row, bf_row,
        q_norm_g[0].reshape(1, -1), kv_norm_g[0].reshape(1, -1),
        _prep_w_in(w_in[0]) + [wuq, wukv] + list(_bias_placement()), tm=INPROJ_ROWS)
    o_all = _attn_call(qt_all, k_all, vt_all, tq=ATTN_TILE, nsub=ATTN_SUBBLOCKS)
    return _tail_call(o_all, g_all, x, g_m, w_o_mla[0].astype(BF16), w_o_fox[0].astype(BF16),
                      w_out[0].astype(BF16), norm_ffn_g[0].reshape(1, d), sc_f, sh_f, g_f, conv_w[0],
                      conv_b[0].reshape(1, -1), norm_final_g.reshape(1, -1),
                      w_up[0].astype(BF16), w_down[0].astype(BF16), tm=TAIL_ROWS,
                      chunk=FFN_CHUNK)
```
